```python
import math
import jax, jax.numpy as jnp
from jax import lax
import numpy as np

D_MODEL = 1024
BATCH = 8
SEQ = 2048
DEPTH = 2
DEC_BATCH = 128
DEC_SEQ = 8
PAST_LEN = 16384
PAGE_SIZE = 128

MIX_WIDTH = D_MODEL
GROUP_WIDTH = MIX_WIDTH // 4

SSD_HEAD_DIM = 32
SSD_HEADS = GROUP_WIDTH // SSD_HEAD_DIM
SSD_GROUPS = 2
SSD_STATE = 128
SSD_CONV = 4
SSD_CHUNK = 128
SSD_CONV_DIM = GROUP_WIDTH + 2 * SSD_GROUPS * SSD_STATE

MLA_HEADS = 4
MLA_NOPE = 64
MLA_ROPE = 32
MLA_V = GROUP_WIDTH // MLA_HEADS
MLA_Q_LORA = 256
MLA_KV_LORA = 128
ROPE_THETA = 10000.0

S5_WIDTH = GROUP_WIDTH
S5_GROUP = 16
S5_GROUPS = S5_WIDTH // S5_GROUP
S5_STATE = 64

SB_HEADS = 4
SB_HEAD_DIM = GROUP_WIDTH // SB_HEADS

Q_BLOCK = 128
FFN_HIDDEN = -(-8 * D_MODEL // (3 * 256)) * 256
PLE_DIM = 256
EPS = 1e-6

IN_SIZES = (GROUP_WIDTH, SSD_CONV_DIM, SSD_HEADS,
            MLA_Q_LORA, MLA_KV_LORA, MLA_ROPE,
            S5_WIDTH,
            GROUP_WIDTH, GROUP_WIDTH, GROUP_WIDTH)
IN_WIDTH = sum(IN_SIZES)

kernel_name = 'hybrid_ssd_mla_s5_stickbreak_decode_step'

F32 = jnp.float32


def rmsnorm(x, g):
    xf = x.astype(F32)
    y = xf * lax.rsqrt(jnp.mean(xf * xf, axis=-1, keepdims=True) + EPS)
    return (y * g.astype(F32)).astype(x.dtype)


def rope_tables(pos):
    inv = ROPE_THETA ** (-jnp.arange(0, MLA_ROPE, 2, dtype=F32) / MLA_ROPE)
    ang = pos.astype(F32)[:, None] * inv[None, :]
    return jnp.cos(ang), jnp.sin(ang)


def apply_rope(x, cos, sin):
    xf = x.astype(F32)
    half = x.shape[-1] // 2
    x1, x2 = xf[..., :half], xf[..., half:]
    return jnp.concatenate([x1 * cos - x2 * sin, x1 * sin + x2 * cos], axis=-1).astype(x.dtype)


def over_query_blocks(fn, q_arrays, q_pos):
    T = q_pos.shape[0]
    if T <= Q_BLOCK or T % Q_BLOCK != 0:
        return fn(*q_arrays, q_pos)
    nb = T // Q_BLOCK
    blocks = tuple(jnp.moveaxis(a.reshape(a.shape[0], nb, Q_BLOCK, *a.shape[2:]), 1, 0) for a in q_arrays)
    out = lax.map(lambda args: fn(*args[:-1], args[-1]), blocks + (q_pos.reshape(nb, Q_BLOCK),))
    out = jnp.moveaxis(out, 0, 1)
    return out.reshape(out.shape[0], T, *out.shape[3:])


def causal_conv(x, prev, w, b):
    xp = jnp.concatenate([prev, x], axis=1)
    T = x.shape[1]
    y = b
    for k in range(SSD_CONV):
        y = y + xp[:, k:k + T] * w[k]
    return jax.nn.silu(y), xp[:, xp.shape[1] - (SSD_CONV - 1):]


def segsum(a):
    T = a.shape[-1]
    ar = jnp.broadcast_to(a[..., :, None], a.shape + (T,))
    strict = jnp.tril(jnp.ones((T, T), dtype=bool), -1)
    ss = jnp.cumsum(jnp.where(strict, ar, 0.0), axis=-2)
    return jnp.where(jnp.tril(jnp.ones((T, T), dtype=bool)), ss, -jnp.inf)


def ssd_scan(xh, dt, A, Bm, Cm, h0):
    Bsz, L, H, P = xh.shape
    Q = SSD_CHUNK if L % SSD_CHUNK == 0 else L
    nc = L // Q
    x = (xh.astype(F32) * dt[..., None]).reshape(Bsz, nc, Q, H, P)
    a = (dt * A).reshape(Bsz, nc, Q, H).transpose(0, 3, 1, 2)
    Bc = Bm.astype(F32).reshape(Bsz, nc, Q, H, -1)
    Cc = Cm.astype(F32).reshape(Bsz, nc, Q, H, -1)
    a_cum = jnp.cumsum(a, axis=-1)
    Lmat = jnp.exp(segsum(a))
    y_diag = jnp.einsum('bclhn,bcshn,bhcls,bcshp->bclhp', Cc, Bc, Lmat, x)
    decay_states = jnp.exp(a_cum[..., -1:] - a_cum)
    states = jnp.einsum('bclhn,bhcl,bclhp->bchpn', Bc, decay_states, x)
    states = jnp.concatenate([h0[:, None], states], axis=1)
    a_last = jnp.pad(a_cum[..., -1], ((0, 0), (0, 0), (1, 0)))
    decay_chunk = jnp.exp(segsum(a_last))
    new_states = jnp.einsum('bhzc,bchpn->bzhpn', decay_chunk, states)
    states, h_final = new_states[:, :-1], new_states[:, -1]
    y_off = jnp.einsum('bclhn,bchpn,bhcl->bclhp', Cc, states, jnp.exp(a_cum))
    return (y_diag + y_off).reshape(Bsz, L, H, P), h_final


def ssd_mixer(z, xbc, dt_raw, conv_prev, h0, conv_w, conv_b, dt_bias, a_log, d_ssd, g_ssd):
    xbc, conv_new = causal_conv(xbc, conv_prev, conv_w, conv_b)
    Bsz, L, _ = xbc.shape
    xs = xbc[..., :GROUP_WIDTH]
    Bm = xbc[..., GROUP_WIDTH:GROUP_WIDTH + SSD_GROUPS * SSD_STATE]
    Cm = xbc[..., GROUP_WIDTH + SSD_GROUPS * SSD_STATE:]
    rep = SSD_HEADS // SSD_GROUPS
    xh = xs.reshape(Bsz, L, SSD_HEADS, SSD_HEAD_DIM)
    Bm = jnp.repeat(Bm.reshape(Bsz, L, SSD_GROUPS, SSD_STATE), rep, axis=2)
    Cm = jnp.repeat(Cm.reshape(Bsz, L, SSD_GROUPS, SSD_STATE), rep, axis=2)
    dt = jax.nn.softplus(dt_raw.astype(F32) + dt_bias.astype(F32))
    A = -jnp.exp(a_log.astype(F32))
    y, h = ssd_scan(xh, dt, A, Bm, Cm, h0.astype(F32))
    y = y + d_ssd.astype(F32)[:, None] * xh.astype(F32)
    y = y.reshape(Bsz, L, GROUP_WIDTH) * jax.nn.silu(z.astype(F32))
    y = rmsnorm(y, g_ssd).astype(z.dtype)
    return y, conv_new, h.astype(h0.dtype)


def mla_attend(q_lat, q_rope, q_pos, k_lat, k_rope, k_pos):
    scale = 1.0 / math.sqrt(MLA_NOPE + MLA_ROPE)
    k_lat32 = k_lat.astype(F32)
    s = (jnp.einsum('bqhc,bkc->bhqk', q_lat.astype(F32), k_lat32)
         + jnp.einsum('bqhr,bkr->bhqk', q_rope.astype(F32), k_rope.astype(F32))) * scale
    s = jnp.where(k_pos[None, None, None, :] <= q_pos[None, None, :, None], s, -jnp.inf)
    p = jax.nn.softmax(s, axis=-1)
    return jnp.einsum('bhqk,bkc->bqhc', p, k_lat32)


def sb_attend(q, q_pos, k, v, k_pos):
    z = jnp.einsum('bqhd,bkhd->bhqk', q.astype(F32), k.astype(F32)) / math.sqrt(SB_HEAD_DIM)
    strict = k_pos[None, None, None, :] < q_pos[None, None, :, None]
    log_keep = jnp.where(strict, jax.nn.log_sigmoid(-z), 0.0)
    after = lax.cumsum(log_keep, axis=3, reverse=True) - log_keep
    w = jnp.where(strict, jnp.exp(jax.nn.log_sigmoid(z) + after), 0.0)
    return jnp.einsum('bhqk,bkhd->bqhd', w, v.astype(F32))


def s5_mixer(u, h0, lam_re, lam_im, b_re, b_im, c_re, c_im, d_s5, log_step, w_glu_a, w_glu_b):
    Bsz, L, _ = u.shape
    lam = lax.complex(lam_re.astype(F32), lam_im.astype(F32))
    step = jnp.exp(log_step.astype(F32))[:, None]
    lam_bar = jnp.exp(lam * step)
    Bc = lax.complex(b_re.astype(F32), b_im.astype(F32))
    Cc = lax.complex(c_re.astype(F32), c_im.astype(F32))
    B_bar = ((lam_bar - 1.0) / lam)[..., None] * Bc
    ug = u.astype(F32).reshape(Bsz, L, S5_GROUPS, S5_GROUP)
    bu = jnp.einsum('gpc,blgc->blgp', B_bar, ug.astype(jnp.complex64))
    h0c = lax.complex(h0[..., 0].astype(F32), h0[..., 1].astype(F32))
    bu = bu.at[:, 0].add(lam_bar * h0c)
    a = jnp.broadcast_to(lam_bar, bu.shape)

    def combine(e1, e2):
        a1, b1 = e1
        a2, b2 = e2
        return a1 * a2, a2 * b1 + b2

    _, hs = lax.associative_scan(combine, (a, bu), axis=1)
    y = jnp.einsum('gcp,blgp->blgc', Cc, hs).real + d_s5.astype(F32).reshape(S5_GROUPS, S5_GROUP) * ug
    y = jax.nn.gelu(y.reshape(Bsz, L, S5_WIDTH)).astype(u.dtype)
    y = (y @ w_glu_a) * jax.nn.sigmoid(y @ w_glu_b)
    h_last = hs[:, -1]
    return y, jnp.stack([h_last.real, h_last.imag], axis=-1).astype(h0.dtype)


def hybrid_layer(x, pemb, q_pos, past_ckv, past_kr, past_k, past_v, conv_prev, ssd_h0, s5_h0, lw):
    (g_mix, w_in, conv_w, conv_b, dt_bias, a_log, d_ssd, g_ssd,
     g_q, w_uq, g_kv, w_uk, w_uv,
     lam_re, lam_im, b_re, b_im, c_re, c_im, d_s5, log_step, w_glu_a, w_glu_b,
     w_out, g_ffn, w_gate, w_up, w_down, g_ple, w_pg, w_ple) = lw
    Bsz, T, _ = x.shape
    h = rmsnorm(x, g_mix)
    proj = h @ w_in
    (z, xbc, dt_raw, cq, ckv_raw, kr_raw, u, sq, sk, sv) = jnp.split(
        proj, np.cumsum(IN_SIZES)[:-1].tolist(), axis=-1)

    y_ssd, conv_new, ssd_h = ssd_mixer(z, xbc, dt_raw, conv_prev, ssd_h0, conv_w, conv_b,
                                       dt_bias, a_log, d_ssd, g_ssd)

    cos, sin = rope_tables(q_pos)
    ckv = rmsnorm(ckv_raw, g_kv)
    kr = apply_rope(kr_raw, cos, sin)
    q = (rmsnorm(cq, g_q) @ w_uq).reshape(Bsz, T, MLA_HEADS, MLA_NOPE + MLA_ROPE)
    q_nope, q_rope = q[..., :MLA_NOPE], q[..., MLA_NOPE:]
    q_rope = apply_rope(q_rope, cos[:, None], sin[:, None])
    q_lat = jnp.einsum('bthd,chd->bthc', q_nope, w_uk)
    k_lat = jnp.concatenate([past_ckv, ckv], axis=1)
    k_r = jnp.concatenate([past_kr, kr], axis=1)
    k_pos = jnp.arange(k_lat.shape[1], dtype=jnp.int32)
    o_lat = over_query_blocks(lambda ql, qr, qp: mla_attend(ql, qr, qp, k_lat, k_r, k_pos),
                              (q_lat, q_rope), q_pos)
    y_mla = jnp.einsum('bthc,chv->bthv', o_lat.astype(x.dtype), w_uv).reshape(Bsz, T, GROUP_WIDTH)

    y_s5, s5_h = s5_mixer(u, s5_h0, lam_re, lam_im, b_re, b_im, c_re, c_im, d_s5, log_step,
                          w_glu_a, w_glu_b)

    sb_q = sq.reshape(Bsz, T, SB_HEADS, SB_HEAD_DIM)
    sb_k = sk.reshape(Bsz, T, SB_HEADS, SB_HEAD_DIM)
    sb_v = sv.reshape(Bsz, T, SB_HEADS, SB_HEAD_DIM)
    k_all = jnp.concatenate([past_k, sb_k], axis=1)
    v_all = jnp.concatenate([past_v, sb_v], axis=1)
    o_sb = over_query_blocks(lambda qq, qp: sb_attend(qq, qp, k_all, v_all, k_pos), (sb_q,), q_pos)
    y_sb = o_sb.astype(x.dtype).reshape(Bsz, T, GROUP_WIDTH)

    x = x + jnp.concatenate([y_ssd, y_mla, y_s5, y_sb], axis=-1) @ w_out
    h2 = rmsnorm(x, g_ffn)
    x = x + (jax.nn.silu(h2 @ w_gate) * (h2 @ w_up)) @ w_down
    gate = jax.nn.sigmoid(rmsnorm(x, g_ple) @ w_pg)
    x = x + (pemb @ w_ple) * gate
    return x, (ckv, kr, sb_k, sb_v, ssd_h, conv_new, s5_h)


def setup_inputs(seed: int = 0) -> dict:
    key = jax.random.key(seed)
    ks = iter(jax.random.split(key, 64))

    def nrm(shape, scale=1.0):
        return scale * jax.random.normal(next(ks), shape, F32)

    def gain(shape):
        return 1.0 + nrm(shape, 0.02)

    n_pages = PAST_LEN // PAGE_SIZE
    n_pool = (DEC_BATCH * n_pages * 5) // 4
    x_prompt = nrm((BATCH, SEQ, D_MODEL))
    x_sample = nrm((DEC_BATCH, DEC_SEQ, D_MODEL))
    p_prompt = nrm((DEPTH, BATCH, SEQ, PLE_DIM))
    p_sample = nrm((DEPTH, DEC_BATCH, DEC_SEQ, PLE_DIM))
    cache_mla_ckv = nrm((DEPTH, n_pool, PAGE_SIZE, MLA_KV_LORA))
    cache_mla_krope = nrm((DEPTH, n_pool, PAGE_SIZE, MLA_ROPE))
    cache_sb_k = nrm((DEPTH, n_pool, PAGE_SIZE, SB_HEADS, SB_HEAD_DIM))
    cache_sb_v = nrm((DEPTH, n_pool, PAGE_SIZE, SB_HEADS, SB_HEAD_DIM))
    state_ssd = nrm((DEPTH, DEC_BATCH, SSD_HEADS, SSD_HEAD_DIM, SSD_STATE), 0.1)
    state_ssd_conv = nrm((DEPTH, DEC_BATCH, SSD_CONV - 1, SSD_CONV_DIM))
    state_s5 = nrm((DEPTH, DEC_BATCH, S5_GROUPS, S5_STATE, 2), 0.1)
    page_table = jax.random.permutation(next(ks), n_pool)[:DEC_BATCH * n_pages].reshape(
        DEC_BATCH, n_pages).astype(jnp.int32)

    dt0 = jnp.exp(jax.random.uniform(next(ks), (DEPTH, SSD_HEADS), F32,
                                     math.log(1e-3), math.log(1e-1)))
    log_step = jax.random.uniform(next(ks), (DEPTH, S5_GROUPS), F32, math.log(1e-3), math.log(1e-1))
    a_log = jnp.log(jax.random.uniform(next(ks), (DEPTH, SSD_HEADS), F32, 1.0, 16.0))
    lam_im = math.pi * jnp.broadcast_to(jnp.arange(S5_STATE, dtype=F32), (DEPTH, S5_GROUPS, S5_STATE))

    return {
        'x_prompt': x_prompt, 'x_sample': x_sample, 'p_prompt': p_prompt, 'p_sample': p_sample,
        'cache_mla_ckv': cache_mla_ckv, 'cache_mla_krope': cache_mla_krope,
        'cache_sb_k': cache_sb_k, 'cache_sb_v': cache_sb_v,
        'state_ssd': state_ssd, 'state_ssd_conv': state_ssd_conv, 'state_s5': state_s5,
        'page_table': page_table,
        'g_mix': gain((DEPTH, D_MODEL)),
        'w_in': nrm((DEPTH, D_MODEL, IN_WIDTH), D_MODEL ** -0.5),
        'conv_w': nrm((DEPTH, SSD_CONV, SSD_CONV_DIM), 0.5),
        'conv_b': nrm((DEPTH, SSD_CONV_DIM), 0.01),
        'dt_bias': dt0 + jnp.log(-jnp.expm1(-dt0)),
        'a_log': a_log,
        'd_ssd': 1.0 + nrm((DEPTH, SSD_HEADS), 0.1),
        'g_ssd': gain((DEPTH, GROUP_WIDTH)),
        'g_q': gain((DEPTH, MLA_Q_LORA)),
        'w_uq': nrm((DEPTH, MLA_Q_LORA, MLA_HEADS * (MLA_NOPE + MLA_ROPE)), MLA_Q_LORA ** -0.5),
        'g_kv': gain((DEPTH, MLA_KV_LORA)),
        'w_uk': nrm((DEPTH, MLA_KV_LORA, MLA_HEADS, MLA_NOPE), MLA_KV_LORA ** -0.5),
        'w_uv': nrm((DEPTH, MLA_KV_LORA, MLA_HEADS, MLA_V), MLA_KV_LORA ** -0.5),
        'lam_re': -0.5 + nrm((DEPTH, S5_GROUPS, S5_STATE), 0.01),
        'lam_im': lam_im + nrm((DEPTH, S5_GROUPS, S5_STATE), 0.01),
        'b_re': nrm((DEPTH, S5_GROUPS, S5_STATE, S5_GROUP), (2 * S5_GROUP) ** -0.5),
        'b_im': nrm((DEPTH, S5_GROUPS, S5_STATE, S5_GROUP), (2 * S5_GROUP) ** -0.5),
        'c_re': nrm((DEPTH, S5_GROUPS, S5_GROUP, S5_STATE), (2 * S5_STATE) ** -0.5),
        'c_im': nrm((DEPTH, S5_GROUPS, S5_GROUP, S5_STATE), (2 * S5_STATE) ** -0.5),
        'd_s5': nrm((DEPTH, S5_WIDTH)),
        'log_step': log_step,
        'w_glu_a': nrm((DEPTH, S5_WIDTH, S5_WIDTH), S5_WIDTH ** -0.5),
        'w_glu_b': nrm((DEPTH, S5_WIDTH, S5_WIDTH), S5_WIDTH ** -0.5),
        'w_out': nrm((DEPTH, MIX_WIDTH, D_MODEL), MIX_WIDTH ** -0.5),
        'g_ffn': gain((DEPTH, D_MODEL)),
        'w_gate': nrm((DEPTH, D_MODEL, FFN_HIDDEN), D_MODEL ** -0.5),
        'w_up': nrm((DEPTH, D_MODEL, FFN_HIDDEN), D_MODEL ** -0.5),
        'w_down': nrm((DEPTH, FFN_HIDDEN, D_MODEL), FFN_HIDDEN ** -0.5),
        'g_ple': gain((DEPTH, D_MODEL)),
        'w_pg': nrm((DEPTH, D_MODEL, D_MODEL), D_MODEL ** -0.5),
        'w_ple': nrm((DEPTH, PLE_DIM, D_MODEL), PLE_DIM ** -0.5),
        'g_final': gain((D_MODEL,)),
    }


def reference(x_prompt, x_sample, p_prompt, p_sample, cache_mla_ckv, cache_mla_krope, cache_sb_k,
              cache_sb_v, state_ssd, state_ssd_conv, state_s5, page_table,
              g_mix, w_in, conv_w, conv_b, dt_bias, a_log, d_ssd, g_ssd, g_q, w_uq, g_kv, w_uk, w_uv,
              lam_re, lam_im, b_re, b_im, c_re, c_im, d_s5, log_step, w_glu_a, w_glu_b,
              w_out, g_ffn, w_gate, w_up, w_down, g_ple, w_pg, w_ple, g_final):
    weights = (g_mix, w_in, conv_w, conv_b, dt_bias, a_log, d_ssd, g_ssd,
               g_q, w_uq, g_kv, w_uk, w_uv,
               lam_re, lam_im, b_re, b_im, c_re, c_im, d_s5, log_step, w_glu_a, w_glu_b,
               w_out, g_ffn, w_gate, w_up, w_down, g_ple, w_pg, w_ple)
    dtp = x_prompt.dtype
    Bp, Tp, _ = x_prompt.shape
    Bs, Ts, _ = x_sample.shape
    n_pages = page_table.shape[1]
    past_len = n_pages * PAGE_SIZE
    pos_prompt = jnp.arange(Tp, dtype=jnp.int32)
    pos_sample = past_len + jnp.arange(Ts, dtype=jnp.int32)

    def gather(pool, i):
        rows = pool[i, page_table]
        return rows.reshape(Bs, past_len, *pool.shape[3:])

    xp, xs = x_prompt, x_sample
    new_p = [[] for _ in range(7)]
    new_s = [[] for _ in range(7)]
    for i in range(DEPTH):
        lw = tuple(w[i] for w in weights)
        xp, st_p = hybrid_layer(
            xp, p_prompt[i], pos_prompt,
            jnp.zeros((Bp, 0, MLA_KV_LORA), dtp), jnp.zeros((Bp, 0, MLA_ROPE), dtp),
            jnp.zeros((Bp, 0, SB_HEADS, SB_HEAD_DIM), dtp), jnp.zeros((Bp, 0, SB_HEADS, SB_HEAD_DIM), dtp),
            jnp.zeros((Bp, SSD_CONV - 1, SSD_CONV_DIM), dtp),
            jnp.zeros((Bp, SSD_HEADS, SSD_HEAD_DIM, SSD_STATE), dtp),
            jnp.zeros((Bp, S5_GROUPS, S5_STATE, 2), dtp), lw)
        xs, st_s = hybrid_layer(
            xs, p_sample[i], pos_sample,
            gather(cache_mla_ckv, i), gather(cache_mla_krope, i),
            gather(cache_sb_k, i), gather(cache_sb_v, i),
            state_ssd_conv[i], state_ssd[i], state_s5[i], lw)
        for lst, a in zip(new_p, st_p):
            lst.append(a)
        for lst, a in zip(new_s, st_s):
            lst.append(a)
    y_prompt = rmsnorm(xp, g_final)
    y_sample = rmsnorm(xs, g_final)
    p_ckv, p_krope, p_sbk, p_sbv, p_ssd, p_conv, p_s5 = [jnp.stack(l) for l in new_p]
    s_ckv, s_krope, s_sbk, s_sbv, s_ssd, s_conv, s_s5 = [jnp.stack(l) for l in new_s]
    return (y_prompt, y_sample, p_ckv, p_krope, p_sbk, p_sbv, p_ssd, p_conv, p_s5,
            s_ckv, s_krope, s_sbk, s_sbv, s_ssd, s_conv, s_s5)
```

```python
import functools
import math

import jax
import jax.numpy as jnp
from jax import lax
from jax.experimental import pallas as pl
from jax.experimental.pallas import tpu as pltpu

F32 = jnp.float32
BF16 = jnp.bfloat16
EPS = 1e-6
ROPE_THETA = 10000.0

LANES = 128
SUBLANES = 8
VMEM_LIMIT = 56 * 1024 * 1024

GROUP_WIDTH = 256
SSD_HEADS = 8
SSD_HEAD_DIM = 32
SSD_GROUPS = 2
SSD_STATE = 128
SSD_CONV = 4
SSD_CHUNK = 128
MLA_HEADS = 4
MLA_NOPE = 64
MLA_ROPE = 32
MLA_KV_LORA = 128
S5_GROUPS = 16
S5_GROUP = 16
S5_STATE = 64
SB_HEADS = 4
SB_HEAD_DIM = 64
NEG_BIG = -1e30


def _dot(a, b):
    return jnp.dot(a, b, preferred_element_type=F32)


def _dot_nt(a, b):
    return lax.dot_general(a, b, (((1,), (1,)), ((), ())), preferred_element_type=F32)


def _dot_tn(a, b):
    return lax.dot_general(a, b, (((0,), (0,)), ((), ())), preferred_element_type=F32)


def _split3(x):
    hi = x.astype(BF16)
    r = x - hi.astype(F32)
    mid = r.astype(BF16)
    lo = (r - mid.astype(F32)).astype(BF16)
    return hi, mid, lo


def _dot3(x, m):
    hi, mid, lo = _split3(x)
    return _dot(hi, m) + _dot(mid, m) + _dot(lo, m)


def _dot3_left(m, x):
    hi, mid, lo = _split3(x)
    return _dot(m, hi) + _dot(m, mid) + _dot(m, lo)


def _rms(x, g):
    return x * lax.rsqrt(jnp.mean(x * x, axis=-1, keepdims=True) + EPS) * g


def _sigmoid(x):
    return 1.0 / (1.0 + jnp.exp(-x))


def _softplus(x):
    return jnp.maximum(x, 0.0) + jnp.log(1.0 + jnp.exp(-jnp.abs(x)))


def _tri(n, kind):
    r = lax.broadcasted_iota(jnp.int32, (n, n), 0)
    c = lax.broadcasted_iota(jnp.int32, (n, n), 1)
    cond = {"lower_incl": c <= r, "row_gt_col": r > c}[kind]
    return jnp.where(cond, 1.0, 0.0).astype(BF16)


def _const_spec(shape):
    nd = len(shape)
    return pl.BlockSpec(shape, lambda *_: (0,) * nd)


def _params(sem):
    return pltpu.CompilerParams(dimension_semantics=sem, vmem_limit_bytes=VMEM_LIMIT)


def _pick(n, cands):
    for c in cands:
        if n % c == 0:
            return c
    raise ValueError(f"no tile in {cands} divides {n}")


def _inproj_kernel(x_ref, tab_ref, gmix_ref, win_ref, gq_ref, wuq_ref, wuk_ref, gkv_ref,
                   z_ref, xbc_ref, dt_ref, ckv_ref, kr_ref, kcat_ref, qcat_ref, u_ref,
                   sqb_ref, skb_ref, svb_ref, sk_ref, sv_ref):
    x = x_ref[...]
    hb = _rms(x, gmix_ref[...]).astype(BF16)
    proj = _dot(hb, win_ref[...])
    z_ref[...] = proj[:, 0:256]
    xbc_ref[...] = proj[:, 256:1024]
    cq = proj[:, 1024:1280]
    ckv_raw = proj[:, 1280:1408]
    u_ref[...] = proj[:, 1408:1664]
    sq = proj[:, 1664:1920]
    sk = proj[:, 1920:2176]
    sv = proj[:, 2176:2432]
    kr_raw = proj[:, 2432:2560]
    kr_rot = proj[:, 2560:2688]
    dt_ref[...] = proj[:, 2688:2816]

    cos = tab_ref[:, 0:LANES]
    sin = tab_ref[:, LANES:2 * LANES]
    kr = kr_raw * cos + kr_rot * sin
    ckv = _rms(ckv_raw, gkv_ref[...])
    ckv_ref[...] = ckv
    kr_ref[...] = kr[:, 0:MLA_ROPE]
    kcat_ref[...] = jnp.concatenate([ckv, kr], axis=-1).astype(BF16)

    cqn = _rms(cq, gq_ref[...]).astype(BF16)
    q = _dot(cqn, wuq_ref[...])
    q_lat = _dot(q[:, 0:256].astype(BF16), wuk_ref[...])
    scale = 1.0 / math.sqrt(MLA_NOPE + MLA_ROPE)
    for h in range(MLA_HEADS):
        lo = 256 + h * LANES
        q_rope = q[:, lo:lo + LANES] * cos + q[:, lo + 512:lo + 512 + LANES] * sin
        qh = jnp.concatenate([q_lat[:, h * LANES:(h + 1) * LANES], q_rope], axis=-1)
        qcat_ref[h] = (qh * scale).astype(BF16)

    sqb_ref[...] = (sq * (1.0 / math.sqrt(SB_HEAD_DIM))).astype(BF16)
    skb_ref[...] = sk.astype(BF16)
    svb_ref[...] = sv.astype(BF16)
    sk_ref[...] = sk
    sv_ref[...] = sv


def _inproj(x, tab, lw, tm):
    n, d = x.shape
    nper = tab.shape[0] // tm
    row = lambda i: (i, 0)
    win, wuq, wuk = lw["w_in_ext"], lw["w_uq_ext"], lw["w_uk_bd"]
    in_specs = [
        pl.BlockSpec((tm, d), row),
        pl.BlockSpec((tm, 2 * LANES), lambda i: (i % nper, 0)),
        _const_spec((1, d)),
        _const_spec(win.shape),
        _const_spec((1, 256)),
        _const_spec(wuq.shape),
        _const_spec(wuk.shape),
        _const_spec((1, MLA_KV_LORA)),
    ]
    outs = [
        ((n, 256), F32), ((n, 768), F32), ((n, LANES), F32), ((n, MLA_KV_LORA), F32),
        ((n, MLA_ROPE), F32), ((n, 256), BF16), ((MLA_HEADS, n, 256), BF16), ((n, 256), F32),
        ((n, 256), BF16), ((n, 256), BF16), ((n, 256), BF16), ((n, 256), F32), ((n, 256), F32),
    ]
    out_specs = []
    for shp, _ in outs:
        if len(shp) == 3:
            out_specs.append(pl.BlockSpec((MLA_HEADS, tm, 256), lambda i: (0, i, 0)))
        else:
            out_specs.append(pl.BlockSpec((tm, shp[1]), row))
    return pl.pallas_call(
        _inproj_kernel,
        grid=(n // tm,),
        in_specs=in_specs,
        out_specs=out_specs,
        out_shape=[jax.ShapeDtypeStruct(s, t) for s, t in outs],
        compiler_params=_params(("parallel",)),
        name="inproj",
    )(x, tab, lw["g_mix"], win, lw["g_q"], wuq, wuk, lw["g_kv"])


def _post_kernel(x_ref, yssd_ref, ymla_ref, ys5_ref, ysb_ref, p_ref, wout_ref, gffn_ref,
                 wg_ref, wu_ref, wd_ref, gple_ref, wpg_ref, wple_ref, gfin_ref,
                 xo_ref, yo_ref, *, hidden_chunk):
    mix = jnp.concatenate([yssd_ref[...], ymla_ref[...], ys5_ref[...], ysb_ref[...]],
                          axis=-1).astype(BF16)
    x1 = x_ref[...] + _dot(mix, wout_ref[...])
    h2 = _rms(x1, gffn_ref[...]).astype(BF16)
    hidden = wg_ref.shape[1]
    acc = jnp.zeros_like(x1)
    for c in range(hidden // hidden_chunk):
        sl = slice(c * hidden_chunk, (c + 1) * hidden_chunk)
        g = _dot(h2, wg_ref[:, sl])
        up = _dot(h2, wu_ref[:, sl])
        a = (g * _sigmoid(g) * up).astype(BF16)
        acc = acc + _dot(a, wd_ref[sl, :])
    x2 = x1 + acc
    gate = _sigmoid(_dot(_rms(x2, gple_ref[...]).astype(BF16), wpg_ref[...]))
    x3 = x2 + _dot(p_ref[...].astype(BF16), wple_ref[...]) * gate
    xo_ref[...] = x3
    yo_ref[...] = _rms(x3, gfin_ref[...])


def _post(x, ys, pemb, lw, g_final, tm):
    n, d = x.shape
    row = lambda i: (i, 0)
    w1 = functools.partial(pl.BlockSpec, pipeline_mode=pl.Buffered(1))
    cmap = lambda i: (0, 0)
    ws = [lw["w_out"], lw["g_ffn"], lw["w_gate"], lw["w_up"], lw["w_down"], lw["g_ple"],
          lw["w_pg"], lw["w_ple"], g_final]
    in_specs = ([pl.BlockSpec((tm, d), row)] + [pl.BlockSpec((tm, 256), row)] * 4
                + [pl.BlockSpec((tm, pemb.shape[1]), row)] + [w1(w.shape, cmap) for w in ws])
    return pl.pallas_call(
        functools.partial(_post_kernel, hidden_chunk=256),
        grid=(n // tm,),
        in_specs=in_specs,
        out_specs=[pl.BlockSpec((tm, d), row)] * 2,
        out_shape=[jax.ShapeDtypeStruct((n, d), F32)] * 2,
        compiler_params=_params(("parallel",)),
        name="post",
    )(x, *ys, pemb, *ws)


def _expand_heads(a, g):
    r = a.shape[0]
    lane_head = lax.broadcasted_iota(jnp.int32, (r, LANES), 1) // SSD_HEAD_DIM
    out = jnp.zeros((r, LANES), F32)
    per = SSD_HEADS // SSD_GROUPS
    for j in range(per):
        h = g * per + j
        out = jnp.where(lane_head == j, jnp.broadcast_to(a[:, h:h + 1], (r, LANES)), out)
    return out


def _ssd_kernel(z_ref, xbc_ref, dt_ref, prev_ref, h0_ref, cw_ref, cb_ref, dtb_ref, alog_ref,
                dexp_ref, g_ref, y_ref, tail_ref, hfin_ref, ht_sc, tail_sc, *, nb, seq_len):
    Q = SSD_CHUNK
    lv = min(seq_len, Q)
    nc = max(seq_len // Q, 1)
    per = SSD_HEADS // SSD_GROUPS
    tri_l = _tri(Q, "lower_incl")
    row_i = lax.broadcasted_iota(jnp.int32, (Q, Q), 0)
    col_i = lax.broadcasted_iota(jnp.int32, (Q, Q), 1)
    causal = col_i <= row_i
    lane_head = lax.broadcasted_iota(jnp.int32, (Q, LANES), 1) // SSD_HEAD_DIM
    sub8 = lax.broadcasted_iota(jnp.int32, (SUBLANES, 768), 0)
    neg_a = -jnp.exp(alog_ref[...])

    def pad_rows(a):
        if lv == Q:
            return a
        return jnp.concatenate([a, jnp.zeros((Q - lv, a.shape[1]), a.dtype)], axis=0)

    def chunk(base):
        raw = pad_rows(xbc_ref[pl.ds(base, lv), :])
        prev8 = tail_sc[...]
        conv = cb_ref[...] + raw * cw_ref[SSD_CONV - 1:SSD_CONV, :]
        for k in range(1, SSD_CONV):
            ra = pltpu.roll(raw, k, 0)
            rp = pltpu.roll(prev8, k, 0)
            top = jnp.where(sub8 < k, rp, ra[0:SUBLANES])
            shifted = top if Q == SUBLANES else jnp.concatenate([top, ra[SUBLANES:]], axis=0)
            conv = conv + shifted * cw_ref[SSD_CONV - 1 - k:SSD_CONV - k, :]
        tail_sc[...] = raw[lv - SUBLANES:lv]
        xbc = conv * _sigmoid(conv)

        dt = _softplus(pad_rows(dt_ref[pl.ds(base, lv), :]) + dtb_ref[...])
        if lv < Q:
            dt = jnp.where(lax.broadcasted_iota(jnp.int32, (Q, LANES), 0) < lv, dt, 0.0)
        a_cum = _dot3_left(tri_l, dt * neg_a)
        a_cum_t = a_cum.T

        ys = []
        for g in range(SSD_GROUPS):
            xs = xbc[:, g * LANES:(g + 1) * LANES]
            bm = xbc[:, 256 + g * LANES:256 + (g + 1) * LANES].astype(BF16)
            cm = xbc[:, 512 + g * LANES:512 + (g + 1) * LANES].astype(BF16)
            dt_e = _expand_heads(dt, g)
            col_e = _expand_heads(a_cum, g)
            last_e = _expand_heads(a_cum[Q - 1:Q, :], g)
            xdt = xs * dt_e
            cb = _dot_nt(cm, bm)
            y = jnp.zeros((Q, LANES), F32)
            for j in range(per):
                h = g * per + j
                seg = a_cum[:, h:h + 1] - a_cum_t[h:h + 1, :]
                lmat = jnp.exp(jnp.where(causal, seg, NEG_BIG))
                xh = jnp.where(lane_head == j, xdt, 0.0).astype(BF16)
                y = y + _dot((cb * lmat).astype(BF16), xh)
            ht = ht_sc[g]
            y = y + _dot(cm, ht.astype(BF16)) * jnp.exp(col_e)
            y = y + dexp_ref[:, g * LANES:(g + 1) * LANES] * xs
            xdec = (xdt * jnp.exp(last_e - col_e)).astype(BF16)
            ht_sc[g] = ht * jnp.exp(last_e) + _dot_tn(bm, xdec)
            ys.append(y)
        y = jnp.concatenate(ys, axis=-1)
        zc = pad_rows(z_ref[pl.ds(base, lv), :])
        y = _rms(y * (zc * _sigmoid(zc)), g_ref[...])
        y_ref[pl.ds(base, lv), :] = y[0:lv]

    def sequence(s, carry):
        tail_sc[...] = prev_ref[s]
        for g in range(SSD_GROUPS):
            ht_sc[g] = h0_ref[s, g * LANES:(g + 1) * LANES, :].T

        def body(c, cc):
            chunk(pl.multiple_of(s * seq_len + c * lv, SUBLANES))
            return cc

        lax.fori_loop(0, nc, body, 0)
        tail_ref[s] = tail_sc[...]
        for g in range(SSD_GROUPS):
            hfin_ref[s, g * LANES:(g + 1) * LANES, :] = ht_sc[g].T
        return carry

    lax.fori_loop(0, nb, sequence, 0)


def _ssd(z, xbc, dt, prev8, h0, lw, nseq, seq_len, nb):
    rows = nb * seq_len
    row = lambda i: (i, 0)
    seq3 = lambda i: (i, 0, 0)
    hp = SSD_HEADS * SSD_HEAD_DIM
    in_specs = [
        pl.BlockSpec((rows, 256), row), pl.BlockSpec((rows, 768), row),
        pl.BlockSpec((rows, LANES), row),
        pl.BlockSpec((nb, SUBLANES, 768), seq3), pl.BlockSpec((nb, hp, SSD_STATE), seq3),
        _const_spec((SSD_CONV, 768)), _const_spec((1, 768)), _const_spec((1, LANES)),
        _const_spec((1, LANES)), _const_spec((1, 256)), _const_spec((1, 256)),
    ]
    out_specs = [pl.BlockSpec((rows, 256), row), pl.BlockSpec((nb, SUBLANES, 768), seq3),
                 pl.BlockSpec((nb, hp, SSD_STATE), seq3)]
    out_shape = [jax.ShapeDtypeStruct((nseq * seq_len, 256), F32),
                 jax.ShapeDtypeStruct((nseq, SUBLANES, 768), F32),
                 jax.ShapeDtypeStruct((nseq, hp, SSD_STATE), F32)]
    return pl.pallas_call(
        functools.partial(_ssd_kernel, nb=nb, seq_len=seq_len),
        grid=(nseq // nb,),
        in_specs=in_specs, out_specs=out_specs, out_shape=out_shape,
        scratch_shapes=[pltpu.VMEM((SSD_GROUPS, SSD_STATE, LANES), F32),
                        pltpu.VMEM((SUBLANES, 768), F32)],
        compiler_params=_params(("parallel",)),
        name="ssd",
    )(z, xbc, dt, prev8, h0, lw["conv_w"], lw["conv_b"], lw["dt_bias"], lw["a_log"],
      lw["d_exp"], lw["g_ssd"])


def _gelu_tanh(x):
    return 0.5 * x * (1.0 + jnp.tanh(math.sqrt(2.0 / math.pi) * (x + 0.044715 * (x * x * x))))


def _s5_kernel(u_ref, h0_ref, lam_ref, bmat_ref, cmat_ref, d_ref, wa_ref, wb_ref,
               y_ref, hlast_ref, hs_sc, h_sc, *, tc):
    ns = S5_GROUPS * S5_STATE
    rows = tc * SUBLANES

    @pl.when(pl.program_id(1) == 0)
    def _():
        h_sc[...] = h0_ref[...]

    u2 = u_ref[...].reshape(rows, 256)
    hs_sc[...] = _dot(u2.astype(BF16), bmat_ref[...]).reshape(tc, SUBLANES, 2 * ns)
    lr = jnp.broadcast_to(lam_ref[0:1, :], (SUBLANES, ns))
    li = jnp.broadcast_to(lam_ref[1:2, :], (SUBLANES, ns))

    def step(t, carry):
        hr, hi = carry
        b = hs_sc[t]
        nr = lr * hr - li * hi + b[:, 0:ns]
        ni = lr * hi + li * hr + b[:, ns:2 * ns]
        hs_sc[t] = jnp.concatenate([nr, ni], axis=-1)
        return nr, ni

    hr, hi = lax.fori_loop(0, tc, step, (h_sc[:, 0:ns], h_sc[:, ns:2 * ns]))
    hcat = jnp.concatenate([hr, hi], axis=-1)
    h_sc[...] = hcat
    hlast_ref[...] = hcat

    hs = hs_sc[...].reshape(rows, 2 * ns).astype(BF16)
    y = _dot(hs, cmat_ref[...]) + d_ref[...] * u2
    yb = _gelu_tanh(y).astype(BF16)
    out = _dot(yb, wa_ref[...]) * _sigmoid(_dot(yb, wb_ref[...]))
    y_ref[...] = out.reshape(tc, SUBLANES, 256)


def _s5(u_tm, h0, lw, tc):
    t, nbatch, _ = u_tm.shape
    ns2 = 2 * S5_GROUPS * S5_STATE
    in_specs = [
        pl.BlockSpec((tc, SUBLANES, 256), lambda b, c: (c, b, 0)),
        pl.BlockSpec((SUBLANES, ns2), lambda b, c: (b, 0)),
        _const_spec((2, ns2 // 2)), _const_spec((256, ns2)), _const_spec((ns2, 256)),
        _const_spec((1, 256)), _const_spec((256, 256)), _const_spec((256, 256)),
    ]
    out_specs = [pl.BlockSpec((tc, SUBLANES, 256), lambda b, c: (c, b, 0)),
                 pl.BlockSpec((SUBLANES, ns2), lambda b, c: (b, 0))]
    return pl.pallas_call(
        functools.partial(_s5_kernel, tc=tc),
        grid=(nbatch // SUBLANES, t // tc),
        in_specs=in_specs, out_specs=out_specs,
        out_shape=[jax.ShapeDtypeStruct((t, nbatch, 256), F32),
                   jax.ShapeDtypeStruct((nbatch, ns2), F32)],
        scratch_shapes=[pltpu.VMEM((tc, SUBLANES, ns2), F32), pltpu.VMEM((SUBLANES, ns2), F32)],
        compiler_params=_params(("parallel", "arbitrary")),
        name="s5",
    )(u_tm, h0, lw["s5_lam"], lw["s5_bmat"], lw["s5_cmat"], lw["d_s5"], lw["w_glu_a"],
      lw["w_glu_b"])


def _mla_prompt_kernel(q_ref, k_ref, wuv_ref, o_ref, m_sc, l_sc, acc_sc, *, tq):
    qi = pl.program_id(1)
    rows = MLA_HEADS * tq
    q = q_ref[...].reshape(rows, 256)
    m_sc[...] = jnp.full(m_sc.shape, -jnp.inf, F32)
    l_sc[...] = jnp.zeros(l_sc.shape, F32)
    acc_sc[...] = jnp.zeros(acc_sc.shape, F32)

    def block(kb, masked):
        k = k_ref[pl.ds(pl.multiple_of(kb * tq, tq), tq), :]
        s = _dot_nt(q, k)
        if masked:
            t_q = lax.broadcasted_iota(jnp.int32, (rows, tq), 0) % tq
            t_k = lax.broadcasted_iota(jnp.int32, (rows, tq), 1)
            s = jnp.where(t_k <= t_q, s, -jnp.inf)
        m_prev = m_sc[:, 0:1]
        m_new = jnp.maximum(m_prev, jnp.max(s, axis=-1, keepdims=True))
        alpha = jnp.exp(m_prev - m_new)
        p = jnp.exp(s - m_new)
        l_sc[...] = jnp.broadcast_to(alpha * l_sc[:, 0:1] + jnp.sum(p, axis=-1, keepdims=True),
                                     l_sc.shape)
        acc_sc[...] = alpha * acc_sc[...] + _dot(p.astype(BF16), k[:, 0:MLA_KV_LORA])
        m_sc[...] = jnp.broadcast_to(m_new, m_sc.shape)

    def body(kb, c):
        block(kb, False)
        return c

    lax.fori_loop(0, qi, body, 0)
    block(qi, True)
    o = (acc_sc[...] / l_sc[...]).astype(BF16)
    ocat = jnp.concatenate([o[h * tq:(h + 1) * tq] for h in range(MLA_HEADS)], axis=-1)
    o_ref[...] = _dot(ocat, wuv_ref[...])


def _mla_prompt(qcat, kcat, wuv, nbatch, t, tq):
    nq = t // tq
    return pl.pallas_call(
        functools.partial(_mla_prompt_kernel, tq=tq),
        grid=(nbatch, nq),
        in_specs=[pl.BlockSpec((MLA_HEADS, tq, 256), lambda b, i: (0, b * nq + i, 0)),
                  pl.BlockSpec((t, 256), lambda b, i: (b, 0)),
                  _const_spec(wuv.shape)],
        out_specs=pl.BlockSpec((tq, 256), lambda b, i: (b * nq + i, 0)),
        out_shape=jax.ShapeDtypeStruct((nbatch * t, 256), F32),
        scratch_shapes=[pltpu.VMEM((MLA_HEADS * tq, LANES), F32)] * 3,
        compiler_params=_params(("parallel", "arbitrary")),
        name="mla_prompt",
    )(qcat, kcat, wuv)


def _log_sigmoid_neg(z):
    return -_softplus(z)


def _head_stack(q):
    lane_head = lax.broadcasted_iota(jnp.int32, q.shape, 1) // SB_HEAD_DIM
    zero = jnp.zeros_like(q)
    return jnp.concatenate([jnp.where(lane_head == h, q, zero) for h in range(SB_HEADS)], axis=0)


def _head_unstack(acc, r):
    lane_head = lax.broadcasted_iota(jnp.int32, (r, 256), 1) // SB_HEAD_DIM
    out = jnp.zeros((r, 256), F32)
    for h in range(SB_HEADS):
        out = jnp.where(lane_head == h, acc[h * r:(h + 1) * r], out)
    return out


def _sb_weights(z, strict, carry, su):
    lk = _log_sigmoid_neg(z)
    if strict is not None:
        lk = jnp.where(strict, lk, 0.0)
    after = _dot3(lk, su) + carry
    w = jnp.exp(z + lk + after)
    if strict is not None:
        w = jnp.where(strict, w, 0.0)
    return w.astype(BF16), carry + jnp.sum(lk, axis=-1, keepdims=True)


def _sb_prompt_kernel(q_ref, k_ref, v_ref, o_ref, acc_sc, carry_sc, *, tq):
    qi = pl.program_id(1)
    rows = SB_HEADS * tq
    qs = _head_stack(q_ref[...])
    su = _tri(tq, "row_gt_col")
    acc_sc[...] = jnp.zeros(acc_sc.shape, F32)
    carry_sc[...] = jnp.zeros(carry_sc.shape, F32)

    def block(kb, masked):
        start = pl.multiple_of(kb * tq, tq)
        k = k_ref[pl.ds(start, tq), :]
        v = v_ref[pl.ds(start, tq), :]
        z = _dot_nt(qs, k)
        strict = None
        if masked:
            t_q = lax.broadcasted_iota(jnp.int32, (rows, tq), 0) % tq
            t_k = lax.broadcasted_iota(jnp.int32, (rows, tq), 1)
            strict = t_k < t_q
        w, carry = _sb_weights(z, strict, carry_sc[:, 0:1], su)
        acc_sc[...] = acc_sc[...] + _dot(w, v)
        carry_sc[...] = jnp.broadcast_to(carry, carry_sc.shape)

    block(qi, True)

    def body(j, c):
        block(qi - 1 - j, False)
        return c

    lax.fori_loop(0, qi, body, 0)
    o_ref[...] = _head_unstack(acc_sc[...], tq)


def _sb_prompt(sq, sk, sv, nbatch, t, tq):
    nq = t // tq
    return pl.pallas_call(
        functools.partial(_sb_prompt_kernel, tq=tq),
        grid=(nbatch, nq),
        in_specs=[pl.BlockSpec((tq, 256), lambda b, i: (b * nq + i, 0)),
                  pl.BlockSpec((t, 256), lambda b, i: (b, 0)),
                  pl.BlockSpec((t, 256), lambda b, i: (b, 0))],
        out_specs=pl.BlockSpec((tq, 256), lambda b, i: (b * nq + i, 0)),
        out_shape=jax.ShapeDtypeStruct((nbatch * t, 256), F32),
        scratch_shapes=[pltpu.VMEM((SB_HEADS * tq, 256), F32),
                        pltpu.VMEM((SB_HEADS * tq, LANES), F32)],
        compiler_params=_params(("parallel", "arbitrary")),
        name="sb_prompt",
    )(sq, sk, sv)


def _decode_kernel(pt_ref, qcat_ref, kcat_ref, sq_ref, sk_ref, sv_ref, wuv_ref, *rest,
                   ppc, ts, page):
    del pt_ref
    ckv_refs = rest[0:ppc]
    kr_refs = rest[ppc:2 * ppc]
    sbk_refs = rest[2 * ppc:3 * ppc]
    sbv_refs = rest[3 * ppc:4 * ppc]
    ymla_ref, ysb_ref = rest[4 * ppc:4 * ppc + 2]
    qlat_sc, qrope_sc, qsb_sc, m_sc, l_sc, acc_sc, sbacc_sc, carry_sc = rest[4 * ppc + 2:]
    c = pl.program_id(1)
    rows = MLA_HEADS * ts
    su = _tri(page, "row_gt_col")

    def mla_update(s_list, v_list):
        m_prev = m_sc[:, 0:1]
        m_new = m_prev
        for s in s_list:
            m_new = jnp.maximum(m_new, jnp.max(s, axis=-1, keepdims=True))
        alpha = jnp.exp(m_prev - m_new)
        l_new = alpha * l_sc[:, 0:1]
        acc = alpha * acc_sc[...]
        for s, v in zip(s_list, v_list):
            p = jnp.exp(s - m_new)
            l_new = l_new + jnp.sum(p, axis=-1, keepdims=True)
            acc = acc + _dot(p.astype(BF16), v)
        m_sc[...] = jnp.broadcast_to(m_new, m_sc.shape)
        l_sc[...] = jnp.broadcast_to(l_new, l_sc.shape)
        acc_sc[...] = acc

    @pl.when(c == 0)
    def _():
        q = qcat_ref[...].reshape(rows, 256)
        qlat = q[:, 0:MLA_KV_LORA].astype(BF16)
        qrope = q[:, MLA_KV_LORA:MLA_KV_LORA + MLA_ROPE].astype(BF16)
        qsb = _head_stack(sq_ref[...]).astype(BF16)
        qlat_sc[...] = qlat
        qrope_sc[...] = qrope
        qsb_sc[...] = qsb
        m_sc[...] = jnp.full(m_sc.shape, -jnp.inf, F32)
        l_sc[...] = jnp.zeros(l_sc.shape, F32)
        acc_sc[...] = jnp.zeros(acc_sc.shape, F32)
        pad = lambda a: jnp.concatenate([a, jnp.zeros((page - ts, a.shape[1]), a.dtype)], axis=0)
        kc = pad(kcat_ref[...])
        t_q = lax.broadcasted_iota(jnp.int32, (rows, page), 0) % ts
        t_k = lax.broadcasted_iota(jnp.int32, (rows, page), 1)
        ckv_new = kc[:, 0:MLA_KV_LORA].astype(BF16)
        kr_new = kc[:, MLA_KV_LORA:MLA_KV_LORA + MLA_ROPE].astype(BF16)
        s = _dot_nt(qlat, ckv_new) + _dot_nt(qrope, kr_new)
        s = jnp.where(t_k <= t_q, s, -jnp.inf)
        mla_update([s], [ckv_new])
        z = _dot_nt(qsb, pad(sk_ref[...]).astype(BF16))
        w, carry = _sb_weights(z, t_k < t_q, jnp.zeros((rows, 1), F32), su)
        sbacc_sc[...] = _dot(w, pad(sv_ref[...]).astype(BF16))
        carry_sc[...] = jnp.broadcast_to(carry, carry_sc.shape)

    qlat = qlat_sc[...]
    qrope = qrope_sc[...]
    qsb = qsb_sc[...]
    s_list, v_list = [], []
    for j in range(ppc):
        ckv = ckv_refs[j][...].astype(BF16)
        kr_t = kr_refs[j][...].astype(BF16)
        s_list.append(_dot_nt(qlat, ckv) + _dot(qrope, kr_t))
        v_list.append(ckv)
    mla_update(s_list, v_list)

    carry = carry_sc[:, 0:1]
    sbacc = sbacc_sc[...]
    for j in reversed(range(ppc)):
        z = _dot(qsb, sbk_refs[j][...].astype(BF16))
        w, carry = _sb_weights(z, None, carry, su)
        sbacc = sbacc + _dot_nt(w, sbv_refs[j][...].astype(BF16))
    sbacc_sc[...] = sbacc
    carry_sc[...] = jnp.broadcast_to(carry, carry_sc.shape)

    @pl.when(c == pl.num_programs(1) - 1)
    def _():
        o = (acc_sc[...] / l_sc[...]).astype(BF16)
        ocat = jnp.concatenate([o[h * ts:(h + 1) * ts] for h in range(MLA_HEADS)], axis=-1)
        ymla_ref[...] = _dot(ocat, wuv_ref[...])
        ysb_ref[...] = _head_unstack(sbacc_sc[...], ts)


def _decode(layer, page_table, qcat, kcat, sq, sk, sv, wuv, caches, ppc):
    cache_ckv, cache_kr, cache_sbk, cache_sbv = caches
    nbatch, n_pages = page_table.shape
    page = cache_ckv.shape[2]
    assert cache_kr.shape[2:] == (MLA_ROPE, page) and cache_sbk.shape[2:] == (GROUP_WIDTH, page)
    ts = qcat.shape[1] // nbatch
    nchunk = n_pages // ppc
    rows = MLA_HEADS * ts

    def page_map(j):
        def index_map(b, c, pt):
            return (layer, pt[b * n_pages + (n_pages - (c + 1) * ppc + j)], 0, 0)
        return index_map

    def cache_specs(cache):
        blk = (None, None) + cache.shape[2:]
        return [pl.BlockSpec(blk, page_map(j)) for j in range(ppc)]

    in_specs = [
        pl.BlockSpec((MLA_HEADS, ts, 256), lambda b, c, pt: (0, b, 0)),
        pl.BlockSpec((ts, 256), lambda b, c, pt: (b, 0)),
        pl.BlockSpec((ts, 256), lambda b, c, pt: (b, 0)),
        pl.BlockSpec((ts, 256), lambda b, c, pt: (b, 0)),
        pl.BlockSpec((ts, 256), lambda b, c, pt: (b, 0)),
        pl.BlockSpec(wuv.shape, lambda b, c, pt: (0, 0)),
    ]
    operands = [qcat, kcat, sq, sk, sv, wuv]
    for cache in caches:
        in_specs += cache_specs(cache)
        operands += [cache] * ppc
    out_spec = pl.BlockSpec((ts, 256), lambda b, c, pt: (b, 0))
    grid_spec = pltpu.PrefetchScalarGridSpec(
        num_scalar_prefetch=1,
        grid=(nbatch, nchunk),
        in_specs=in_specs,
        out_specs=[out_spec, out_spec],
        scratch_shapes=[
            pltpu.VMEM((rows, MLA_KV_LORA), BF16), pltpu.VMEM((rows, MLA_ROPE), BF16),
            pltpu.VMEM((rows, 256), BF16),
            pltpu.VMEM((rows, LANES), F32), pltpu.VMEM((rows, LANES), F32),
            pltpu.VMEM((rows, MLA_KV_LORA), F32), pltpu.VMEM((rows, 256), F32),
            pltpu.VMEM((rows, LANES), F32),
        ],
    )
    return pl.pallas_call(
        functools.partial(_decode_kernel, ppc=ppc, ts=ts, page=page),
        grid_spec=grid_spec,
        out_shape=[jax.ShapeDtypeStruct((nbatch * ts, 256), F32)] * 2,
        compiler_params=_params(("parallel", "arbitrary")),
        name="decode",
    )(page_table.reshape(-1), *operands)


def _pad_lanes(w, width=LANES):
    return jnp.pad(w, ((0, 0), (0, width - w.shape[1])))


def _rot_cols(w):
    half = w.shape[-1] // 2
    return jnp.concatenate([-w[..., half:], w[..., :half]], axis=-1)


def _prep_layer(i, p):
    gw = GROUP_WIDTH
    conv_dim = gw + 2 * SSD_GROUPS * SSD_STATE
    sizes = (gw, conv_dim, SSD_HEADS, 256, MLA_KV_LORA, MLA_ROPE, gw, gw, gw, gw)
    offs = [0]
    for s in sizes:
        offs.append(offs[-1] + s)
    w_in = p["w_in"][i]
    z, xbc, dt, cq, ckv, kr, u, sq, sk, sv = [w_in[:, offs[j]:offs[j + 1]] for j in range(10)]
    w_in_ext = jnp.concatenate(
        [z, xbc, cq, ckv, u, sq, sk, sv, _pad_lanes(kr), _pad_lanes(_rot_cols(kr)), _pad_lanes(dt)],
        axis=1).astype(BF16)

    w_uq = p["w_uq"][i].reshape(256, MLA_HEADS, MLA_NOPE + MLA_ROPE)
    nope = w_uq[:, :, :MLA_NOPE].reshape(256, MLA_HEADS * MLA_NOPE)
    rope = w_uq[:, :, MLA_NOPE:]
    padh = lambda w: jnp.pad(w, ((0, 0), (0, 0), (0, LANES - MLA_ROPE))).reshape(256, MLA_HEADS * LANES)
    w_uq_ext = jnp.concatenate([nope, padh(rope), padh(_rot_cols(rope))], axis=1).astype(BF16)
    eye_h = jnp.eye(MLA_HEADS, dtype=F32)
    w_uk_bd = jnp.einsum("chd,hg->hdgc", p["w_uk"][i], eye_h).reshape(
        MLA_HEADS * MLA_NOPE, MLA_HEADS * MLA_KV_LORA).astype(BF16)
    w_uv_bd = jnp.einsum("chv,hg->hcgv", p["w_uv"][i], eye_h).reshape(
        MLA_HEADS * MLA_KV_LORA, GROUP_WIDTH).astype(BF16)

    lr, li = p["lam_re"][i], p["lam_im"][i]
    step = jnp.exp(p["log_step"][i])[:, None]
    mag = jnp.exp(lr * step)
    br, bi = mag * jnp.cos(li * step), mag * jnp.sin(li * step)
    den = lr * lr + li * li
    cr = ((br - 1.0) * lr + bi * li) / den
    ci = (bi * lr - (br - 1.0) * li) / den
    b_re, b_im = p["b_re"][i], p["b_im"][i]
    bbar_re = cr[..., None] * b_re - ci[..., None] * b_im
    bbar_im = cr[..., None] * b_im + ci[..., None] * b_re
    eye_g = jnp.eye(S5_GROUPS, dtype=F32)
    ns = S5_GROUPS * S5_STATE
    to_b = lambda b: jnp.einsum("gpc,gh->gchp", b, eye_g).reshape(S5_GROUPS * S5_GROUP, ns)
    s5_bmat = jnp.concatenate([to_b(bbar_re), to_b(bbar_im)], axis=1).astype(BF16)
    to_c = lambda c: jnp.einsum("gcp,gh->hpgc", c, eye_g).reshape(ns, S5_GROUPS * S5_GROUP)
    s5_cmat = jnp.concatenate([to_c(p["c_re"][i]), -to_c(p["c_im"][i])], axis=0).astype(BF16)
    s5_lam = jnp.stack([br.reshape(ns), bi.reshape(ns)])

    row = lambda v: v.reshape(1, -1)
    return {
        "g_mix": row(p["g_mix"][i]), "w_in_ext": w_in_ext, "g_q": row(p["g_q"][i]),
        "w_uq_ext": w_uq_ext, "w_uk_bd": w_uk_bd, "w_uv_bd": w_uv_bd, "g_kv": row(p["g_kv"][i]),
        "conv_w": p["conv_w"][i], "conv_b": row(p["conv_b"][i]),
        "dt_bias": _pad_lanes(row(p["dt_bias"][i])), "a_log": _pad_lanes(row(p["a_log"][i])),
        "d_exp": row(jnp.repeat(p["d_ssd"][i], SSD_HEAD_DIM)), "g_ssd": row(p["g_ssd"][i]),
        "s5_lam": s5_lam, "s5_bmat": s5_bmat, "s5_cmat": s5_cmat, "d_s5": row(p["d_s5"][i]),
        "w_glu_a": p["w_glu_a"][i].astype(BF16), "w_glu_b": p["w_glu_b"][i].astype(BF16),
        "w_out": p["w_out"][i].astype(BF16), "g_ffn": row(p["g_ffn"][i]),
        "w_gate": p["w_gate"][i].astype(BF16), "w_up": p["w_up"][i].astype(BF16),
        "w_down": p["w_down"][i].astype(BF16), "g_ple": row(p["g_ple"][i]),
        "w_pg": p["w_pg"][i].astype(BF16), "w_ple": p["w_ple"][i].astype(BF16),
    }


def _rope_table(pos):
    inv = ROPE_THETA ** (-jnp.arange(0, MLA_ROPE, 2, dtype=F32) / MLA_ROPE)
    ang = pos.astype(F32)[:, None] * inv[None, :]
    cos, sin = jnp.cos(ang), jnp.sin(ang)
    tile = lambda a: jnp.tile(jnp.concatenate([a, a], axis=-1), (1, LANES // MLA_ROPE))
    return jnp.concatenate([tile(cos), tile(sin)], axis=-1)


def _s5_state_in(h):
    b = h.shape[0]
    return jnp.moveaxis(h.reshape(b, S5_GROUPS * S5_STATE, 2), 2, 1).reshape(b, -1)


def _s5_state_out(h):
    b = h.shape[0]
    return jnp.moveaxis(h.reshape(b, 2, S5_GROUPS * S5_STATE), 1, 2).reshape(
        b, S5_GROUPS, S5_STATE, 2)


def _group_step(x, pemb, tab, lw, g_final, *, nseq, seq_len, conv_prev, ssd_h0, s5_h0,
                attend, tm, ssd_nb, s5_tc):
    (z, xbc, dt, ckv, kr, kcat, qcat, u, sqb, skb, svb, sk, sv) = _inproj(x, tab, lw, tm)
    hp = SSD_HEADS * SSD_HEAD_DIM
    prev8 = jnp.pad(conv_prev, ((0, 0), (SUBLANES - (SSD_CONV - 1), 0), (0, 0)))
    y_ssd, tail, ssd_h = _ssd(z, xbc, dt, prev8, ssd_h0.reshape(nseq, hp, SSD_STATE), lw,
                              nseq, seq_len, ssd_nb)
    u_tm = jnp.swapaxes(u.reshape(nseq, seq_len, 256), 0, 1)
    y_s5_tm, s5_h = _s5(u_tm, _s5_state_in(s5_h0), lw, s5_tc)
    y_s5 = jnp.swapaxes(y_s5_tm, 0, 1).reshape(nseq * seq_len, 256)
    y_mla, y_sb = attend(qcat, kcat, sqb, skb, svb, sk, sv)
    x_new, y_fin = _post(x, (y_ssd, y_mla, y_s5, y_sb), pemb, lw, g_final, tm)
    states = (
        ckv.reshape(nseq, seq_len, MLA_KV_LORA), kr.reshape(nseq, seq_len, MLA_ROPE),
        sk.reshape(nseq, seq_len, SB_HEADS, SB_HEAD_DIM),
        sv.reshape(nseq, seq_len, SB_HEADS, SB_HEAD_DIM),
        ssd_h.reshape(nseq, SSD_HEADS, SSD_HEAD_DIM, SSD_STATE),
        tail[:, SUBLANES - (SSD_CONV - 1):, :],
        _s5_state_out(s5_h),
    )
    return x_new, y_fin, states


def kernel(x_prompt, x_sample, p_prompt, p_sample, cache_mla_ckv, cache_mla_krope, cache_sb_k, cache_sb_v, state_ssd, state_ssd_conv, state_s5, page_table, g_mix, w_in, conv_w, conv_b, dt_bias, a_log, d_ssd, g_ssd, g_q, w_uq, g_kv, w_uk, w_uv, lam_re, lam_im, b_re, b_im, c_re, c_im, d_s5, log_step, w_glu_a, w_glu_b, w_out, g_ffn, w_gate, w_up, w_down, g_ple, w_pg, w_ple, g_final):
    params = dict(g_mix=g_mix, w_in=w_in, conv_w=conv_w, conv_b=conv_b, dt_bias=dt_bias,
                  a_log=a_log, d_ssd=d_ssd, g_ssd=g_ssd, g_q=g_q, w_uq=w_uq, g_kv=g_kv,
                  w_uk=w_uk, w_uv=w_uv, lam_re=lam_re, lam_im=lam_im, b_re=b_re, b_im=b_im,
                  c_re=c_re, c_im=c_im, d_s5=d_s5, log_step=log_step, w_glu_a=w_glu_a,
                  w_glu_b=w_glu_b, w_out=w_out, g_ffn=g_ffn, w_gate=w_gate, w_up=w_up,
                  w_down=w_down, g_ple=g_ple, w_pg=w_pg, w_ple=w_ple)
    bp, tp, d = x_prompt.shape
    bs, ts, _ = x_sample.shape
    depth = w_in.shape[0]
    n_pages = page_table.shape[1]
    page = cache_mla_ckv.shape[2]
    past_len = n_pages * page
    n_p, n_s = bp * tp, bs * ts
    assert ts == SUBLANES and tp % SSD_CHUNK == 0 and bp % SUBLANES == 0 and bs % SUBLANES == 0

    tm_p = _pick(tp, (256, 128))
    tm_s = _pick(n_s, (256, 128, 64))
    tq = _pick(tp, (256, 128))
    ppc = _pick(n_pages, (8, 4, 2, 1))
    ssd_nb_s = _pick(bs, (16, 8))
    s5_tc_p = _pick(tp, (64, 32))

    tab_p = _rope_table(jnp.arange(tp, dtype=jnp.int32))
    tab_s = jnp.tile(_rope_table(past_len + jnp.arange(ts, dtype=jnp.int32)), (tm_s // ts, 1))
    g_fin = g_final.reshape(1, d)
    kv_t = lambda c: jnp.transpose(c, (0, 1, 3, 4, 2)).reshape(c.shape[:2] + (GROUP_WIDTH, page))
    caches = (cache_mla_ckv, jnp.transpose(cache_mla_krope, (0, 1, 3, 2)),
              kv_t(cache_sb_k), kv_t(cache_sb_v))

    xp = x_prompt.reshape(n_p, d)
    xs = x_sample.reshape(n_s, d)
    new_p, new_s = [], []
    yp = ys = None
    for i in range(depth):
        lw = _prep_layer(i, params)

        def attend_prompt(qcat, kcat, sqb, skb, svb, sk, sv, lw=lw):
            return (_mla_prompt(qcat, kcat, lw["w_uv_bd"], bp, tp, tq),
                    _sb_prompt(sqb, skb, svb, bp, tp, tq))

        def attend_sample(qcat, kcat, sqb, skb, svb, sk, sv, lw=lw, i=i):
            return _decode(i, page_table, qcat.astype(F32), kcat.astype(F32), sqb.astype(F32),
                           sk, sv, lw["w_uv_bd"], caches, ppc)

        xp, yp, st_p = _group_step(
            xp, p_prompt[i].reshape(n_p, -1), tab_p, lw, g_fin, nseq=bp, seq_len=tp,
            conv_prev=jnp.zeros((bp, SSD_CONV - 1, 768), F32),
            ssd_h0=jnp.zeros((bp, SSD_HEADS, SSD_HEAD_DIM, SSD_STATE), F32),
            s5_h0=jnp.zeros((bp, S5_GROUPS, S5_STATE, 2), F32),
            attend=attend_prompt, tm=tm_p, ssd_nb=1, s5_tc=s5_tc_p)
        xs, ys, st_s = _group_step(
            xs, p_sample[i].reshape(n_s, -1), tab_s, lw, g_fin, nseq=bs, seq_len=ts,
            conv_prev=state_ssd_conv[i], ssd_h0=state_ssd[i], s5_h0=state_s5[i],
            attend=attend_sample, tm=tm_s, ssd_nb=ssd_nb_s, s5_tc=ts)
        new_p.append(st_p)
        new_s.append(st_s)

    stack = lambda lst: tuple(jnp.stack([st[j] for st in lst]) for j in range(7))
    return ((yp.reshape(bp, tp, d), ys.reshape(bs, ts, d)) + stack(new_p) + stack(new_s))
```

```python
import functools
import math

import jax
import jax.numpy as jnp
from jax import lax
from jax.experimental import pallas as pl
from jax.experimental.pallas import tpu as pltpu

F32 = jnp.float32
BF16 = jnp.bfloat16
EPS = 1e-6
ROPE_THETA = 10000.0

LANES = 128
SUBLANES = 8
VMEM_LIMIT = 56 * 1024 * 1024

GROUP_WIDTH = 256
SSD_HEADS = 8
SSD_HEAD_DIM = 32
SSD_GROUPS = 2
SSD_STATE = 128
SSD_CONV = 4
SSD_CHUNK = 128
MLA_HEADS = 4
MLA_NOPE = 64
MLA_ROPE = 32
MLA_KV_LORA = 128
S5_GROUPS = 16
S5_GROUP = 16
S5_STATE = 64
SB_HEADS = 4
SB_HEAD_DIM = 64
NEG_BIG = -1e30


def _dot(a, b):
    return jnp.dot(a, b, preferred_element_type=F32)


def _dot_nt(a, b):
    return lax.dot_general(a, b, (((1,), (1,)), ((), ())), preferred_element_type=F32)


def _dot_tn(a, b):
    return lax.dot_general(a, b, (((0,), (0,)), ((), ())), preferred_element_type=F32)


def _split3(x):
    hi = x.astype(BF16)
    r = x - hi.astype(F32)
    mid = r.astype(BF16)
    lo = (r - mid.astype(F32)).astype(BF16)
    return hi, mid, lo


def _dot3(x, m):
    hi, mid, lo = _split3(x)
    return _dot(hi, m) + _dot(mid, m) + _dot(lo, m)


def _dot3_left(m, x):
    hi, mid, lo = _split3(x)
    return _dot(m, hi) + _dot(m, mid) + _dot(m, lo)


def _rms(x, g):
    return x * lax.rsqrt(jnp.mean(x * x, axis=-1, keepdims=True) + EPS) * g


def _sigmoid(x):
    return 1.0 / (1.0 + jnp.exp(-x))


def _softplus(x):
    return jnp.maximum(x, 0.0) + jnp.log(1.0 + jnp.exp(-jnp.abs(x)))


def _tri(n, kind):
    r = lax.broadcasted_iota(jnp.int32, (n, n), 0)
    c = lax.broadcasted_iota(jnp.int32, (n, n), 1)
    cond = {"lower_incl": c <= r, "row_gt_col": r > c}[kind]
    return jnp.where(cond, 1.0, 0.0).astype(BF16)


def _const_spec(shape):
    nd = len(shape)
    return pl.BlockSpec(shape, lambda *_: (0,) * nd)


def _tile_lanes(a, width):
    reps = width // a.shape[1]
    return a if reps == 1 else jnp.concatenate([a] * reps, axis=-1)


def _params(sem):
    return pltpu.CompilerParams(dimension_semantics=sem, vmem_limit_bytes=VMEM_LIMIT)


def _pick(n, cands):
    for c in cands:
        if n % c == 0:
            return c
    raise ValueError(f"no tile in {cands} divides {n}")


def _inproj_kernel(x_ref, tab_ref, gmix_ref, win_ref, gq_ref, wuq_ref, wuk_ref, gkv_ref,
                   z_ref, xbc_ref, dt_ref, ckv_ref, kr_ref, kcat_ref, qcat_ref, u_ref,
                   sqb_ref, skb_ref, svb_ref, sk_ref, sv_ref):
    x = x_ref[...]
    hb = _rms(x, gmix_ref[...]).astype(BF16)
    proj = _dot(hb, win_ref[...])
    z_ref[...] = proj[:, 0:256]
    xbc_ref[...] = proj[:, 256:1024]
    cq = proj[:, 1024:1280]
    ckv_raw = proj[:, 1280:1408]
    u_ref[...] = proj[:, 1408:1664]
    sq = proj[:, 1664:1920]
    sk = proj[:, 1920:2176]
    sv = proj[:, 2176:2432]
    kr_raw = proj[:, 2432:2560]
    kr_rot = proj[:, 2560:2688]
    dt_ref[...] = proj[:, 2688:2816]

    cos = tab_ref[:, 0:LANES]
    sin = tab_ref[:, LANES:2 * LANES]
    kr = kr_raw * cos + kr_rot * sin
    ckv = _rms(ckv_raw, gkv_ref[...])
    ckv_ref[...] = ckv
    kr_ref[...] = kr[:, 0:MLA_ROPE]
    kcat_ref[...] = jnp.concatenate([ckv, kr], axis=-1).astype(BF16)

    cqn = _rms(cq, gq_ref[...]).astype(BF16)
    q = _dot(cqn, wuq_ref[...])
    q_lat = _dot(q[:, 0:256].astype(BF16), wuk_ref[...])
    scale = 1.0 / math.sqrt(MLA_NOPE + MLA_ROPE)
    for h in range(MLA_HEADS):
        lo = 256 + h * LANES
        q_rope = q[:, lo:lo + LANES] * cos + q[:, lo + 512:lo + 512 + LANES] * sin
        qh = jnp.concatenate([q_lat[:, h * LANES:(h + 1) * LANES], q_rope], axis=-1)
        qcat_ref[h] = (qh * scale).astype(BF16)

    sqb_ref[...] = (sq * (1.0 / math.sqrt(SB_HEAD_DIM))).astype(BF16)
    skb_ref[...] = sk.astype(BF16)
    svb_ref[...] = sv.astype(BF16)
    sk_ref[...] = sk
    sv_ref[...] = sv


def _inproj(x, tab, lw, tm):
    n, d = x.shape
    nper = tab.shape[0] // tm
    row = lambda i: (i, 0)
    win, wuq, wuk = lw["w_in_ext"], lw["w_uq_ext"], lw["w_uk_bd"]
    in_specs = [
        pl.BlockSpec((tm, d), row),
        pl.BlockSpec((tm, 2 * LANES), lambda i: (i % nper, 0)),
        _const_spec((1, d)),
        _const_spec(win.shape),
        _const_spec((1, 256)),
        _const_spec(wuq.shape),
        _const_spec(wuk.shape),
        _const_spec((1, MLA_KV_LORA)),
    ]
    outs = [
        ((n, 256), F32), ((n, 768), F32), ((n, LANES), F32), ((n, MLA_KV_LORA), F32),
        ((n, MLA_ROPE), F32), ((n, 256), BF16), ((MLA_HEADS, n, 256), BF16), ((n, 256), F32),
        ((n, 256), BF16), ((n, 256), BF16), ((n, 256), BF16), ((n, 256), F32), ((n, 256), F32),
    ]
    out_specs = []
    for shp, _ in outs:
        if len(shp) == 3:
            out_specs.append(pl.BlockSpec((MLA_HEADS, tm, 256), lambda i: (0, i, 0)))
        else:
            out_specs.append(pl.BlockSpec((tm, shp[1]), row))
    return pl.pallas_call(
        _inproj_kernel,
        grid=(n // tm,),
        in_specs=in_specs,
        out_specs=out_specs,
        out_shape=[jax.ShapeDtypeStruct(s, t) for s, t in outs],
        compiler_params=_params(("parallel",)),
        name="inproj",
    )(x, tab, lw["g_mix"], win, lw["g_q"], wuq, wuk, lw["g_kv"])


def _post_kernel(x_ref, yssd_ref, ymla_ref, ys5_ref, ysb_ref, p_ref, wout_ref, gffn_ref,
                 wg_ref, wu_ref, wd_ref, gple_ref, wpg_ref, wple_ref, gfin_ref,
                 xo_ref, yo_ref, *, hidden_chunk):
    mix = jnp.concatenate([yssd_ref[...], ymla_ref[...], ys5_ref[...], ysb_ref[...]],
                          axis=-1).astype(BF16)
    x1 = x_ref[...] + _dot(mix, wout_ref[...])
    h2 = _rms(x1, gffn_ref[...]).astype(BF16)
    hidden = wg_ref.shape[1]
    acc = jnp.zeros_like(x1)
    for c in range(hidden // hidden_chunk):
        sl = slice(c * hidden_chunk, (c + 1) * hidden_chunk)
        g = _dot(h2, wg_ref[:, sl])
        up = _dot(h2, wu_ref[:, sl])
        a = (g * _sigmoid(g) * up).astype(BF16)
        acc = acc + _dot(a, wd_ref[sl, :])
    x2 = x1 + acc
    gate = _sigmoid(_dot(_rms(x2, gple_ref[...]).astype(BF16), wpg_ref[...]))
    x3 = x2 + _dot(p_ref[...].astype(BF16), wple_ref[...]) * gate
    xo_ref[...] = x3
    yo_ref[...] = _rms(x3, gfin_ref[...])


def _post(x, ys, pemb, lw, g_final, tm):
    n, d = x.shape
    row = lambda i: (i, 0)
    w1 = functools.partial(pl.BlockSpec, pipeline_mode=pl.Buffered(1))
    cmap = lambda i: (0, 0)
    ws = [lw["w_out"], lw["g_ffn"], lw["w_gate"], lw["w_up"], lw["w_down"], lw["g_ple"],
          lw["w_pg"], lw["w_ple"], g_final]
    in_specs = ([pl.BlockSpec((tm, d), row)] + [pl.BlockSpec((tm, 256), row)] * 4
                + [pl.BlockSpec((tm, pemb.shape[1]), row)] + [w1(w.shape, cmap) for w in ws])
    return pl.pallas_call(
        functools.partial(_post_kernel, hidden_chunk=256),
        grid=(n // tm,),
        in_specs=in_specs,
        out_specs=[pl.BlockSpec((tm, d), row)] * 2,
        out_shape=[jax.ShapeDtypeStruct((n, d), F32)] * 2,
        compiler_params=_params(("parallel",)),
        name="post",
    )(x, *ys, pemb, *ws)


def _expand_heads(a, g):
    r = a.shape[0]
    lane_head = lax.broadcasted_iota(jnp.int32, (r, LANES), 1) // SSD_HEAD_DIM
    out = jnp.zeros((r, LANES), F32)
    per = SSD_HEADS // SSD_GROUPS
    for j in range(per):
        h = g * per + j
        out = jnp.where(lane_head == j, jnp.broadcast_to(a[:, h:h + 1], (r, LANES)), out)
    return out


def _ssd_kernel(z_ref, xbc_ref, dt_ref, prev_ref, h0_ref, cw_ref, cb_ref, dtb_ref, alog_ref,
                dexp_ref, g_ref, y_ref, tail_ref, hfin_ref, ht_sc, tail_sc, *, nb, seq_len):
    Q = SSD_CHUNK
    lv = min(seq_len, Q)
    nc = max(seq_len // Q, 1)
    per = SSD_HEADS // SSD_GROUPS
    tri_l = _tri(Q, "lower_incl")
    row_i = lax.broadcasted_iota(jnp.int32, (Q, Q), 0)
    col_i = lax.broadcasted_iota(jnp.int32, (Q, Q), 1)
    causal = col_i <= row_i
    lane_head = lax.broadcasted_iota(jnp.int32, (Q, LANES), 1) // SSD_HEAD_DIM
    sub8 = lax.broadcasted_iota(jnp.int32, (SUBLANES, 768), 0)
    neg_a = -jnp.exp(alog_ref[...])

    def pad_rows(a):
        if lv == Q:
            return a
        return jnp.concatenate([a, jnp.zeros((Q - lv, a.shape[1]), a.dtype)], axis=0)

    def chunk(base):
        raw = pad_rows(xbc_ref[pl.ds(base, lv), :])
        prev8 = tail_sc[...]
        conv = cb_ref[...] + raw * cw_ref[SSD_CONV - 1:SSD_CONV, :]
        for k in range(1, SSD_CONV):
            ra = pltpu.roll(raw, k, 0)
            rp = pltpu.roll(prev8, k, 0)
            top = jnp.where(sub8 < k, rp, ra[0:SUBLANES])
            shifted = top if Q == SUBLANES else jnp.concatenate([top, ra[SUBLANES:]], axis=0)
            conv = conv + shifted * cw_ref[SSD_CONV - 1 - k:SSD_CONV - k, :]
        tail_sc[...] = raw[lv - SUBLANES:lv]
        xbc = conv * _sigmoid(conv)

        dt = _softplus(pad_rows(dt_ref[pl.ds(base, lv), :]) + dtb_ref[...])
        if lv < Q:
            dt = jnp.where(lax.broadcasted_iota(jnp.int32, (Q, LANES), 0) < lv, dt, 0.0)
        a_cum = _dot3_left(tri_l, dt * neg_a)
        a_cum_t = a_cum.T

        ys = []
        for g in range(SSD_GROUPS):
            xs = xbc[:, g * LANES:(g + 1) * LANES]
            bm = xbc[:, 256 + g * LANES:256 + (g + 1) * LANES].astype(BF16)
            cm = xbc[:, 512 + g * LANES:512 + (g + 1) * LANES].astype(BF16)
            dt_e = _expand_heads(dt, g)
            col_e = _expand_heads(a_cum, g)
            last_e = _expand_heads(a_cum[Q - 1:Q, :], g)
            xdt = xs * dt_e
            cb = _dot_nt(cm, bm)
            y = jnp.zeros((Q, LANES), F32)
            for j in range(per):
                h = g * per + j
                seg = a_cum[:, h:h + 1] - a_cum_t[h:h + 1, :]
                lmat = jnp.exp(jnp.where(causal, seg, NEG_BIG))
                xh = jnp.where(lane_head == j, xdt, 0.0).astype(BF16)
                y = y + _dot((cb * lmat).astype(BF16), xh)
            ht = ht_sc[g]
            y = y + _dot(cm, ht.astype(BF16)) * jnp.exp(col_e)
            y = y + dexp_ref[:, g * LANES:(g + 1) * LANES] * xs
            xdec = (xdt * jnp.exp(last_e - col_e)).astype(BF16)
            ht_sc[g] = ht * jnp.exp(last_e) + _dot_tn(bm, xdec)
            ys.append(y)
        y = jnp.concatenate(ys, axis=-1)
        zc = pad_rows(z_ref[pl.ds(base, lv), :])
        y = _rms(y * (zc * _sigmoid(zc)), g_ref[...])
        y_ref[pl.ds(base, lv), :] = y[0:lv]

    def sequence(s, carry):
        tail_sc[...] = prev_ref[s]
        for g in range(SSD_GROUPS):
            ht_sc[g] = h0_ref[s, g * LANES:(g + 1) * LANES, :].T

        def body(c, cc):
            chunk(pl.multiple_of(s * seq_len + c * lv, SUBLANES))
            return cc

        lax.fori_loop(0, nc, body, 0)
        tail_ref[s] = tail_sc[...]
        for g in range(SSD_GROUPS):
            hfin_ref[s, g * LANES:(g + 1) * LANES, :] = ht_sc[g].T
        return carry

    lax.fori_loop(0, nb, sequence, 0)


def _ssd(z, xbc, dt, prev8, h0, lw, nseq, seq_len, nb):
    rows = nb * seq_len
    row = lambda i: (i, 0)
    seq3 = lambda i: (i, 0, 0)
    hp = SSD_HEADS * SSD_HEAD_DIM
    in_specs = [
        pl.BlockSpec((rows, 256), row), pl.BlockSpec((rows, 768), row),
        pl.BlockSpec((rows, LANES), row),
        pl.BlockSpec((nb, SUBLANES, 768), seq3), pl.BlockSpec((nb, hp, SSD_STATE), seq3),
        _const_spec((SSD_CONV, 768)), _const_spec((1, 768)), _const_spec((1, LANES)),
        _const_spec((1, LANES)), _const_spec((1, 256)), _const_spec((1, 256)),
    ]
    out_specs = [pl.BlockSpec((rows, 256), row), pl.BlockSpec((nb, SUBLANES, 768), seq3),
                 pl.BlockSpec((nb, hp, SSD_STATE), seq3)]
    out_shape = [jax.ShapeDtypeStruct((nseq * seq_len, 256), F32),
                 jax.ShapeDtypeStruct((nseq, SUBLANES, 768), F32),
                 jax.ShapeDtypeStruct((nseq, hp, SSD_STATE), F32)]
    return pl.pallas_call(
        functools.partial(_ssd_kernel, nb=nb, seq_len=seq_len),
        grid=(nseq // nb,),
        in_specs=in_specs, out_specs=out_specs, out_shape=out_shape,
        scratch_shapes=[pltpu.VMEM((SSD_GROUPS, SSD_STATE, LANES), F32),
                        pltpu.VMEM((SUBLANES, 768), F32)],
        compiler_params=_params(("parallel",)),
        name="ssd",
    )(z, xbc, dt, prev8, h0, lw["conv_w"], lw["conv_b"], lw["dt_bias"], lw["a_log"],
      lw["d_exp"], lw["g_ssd"])


def _gelu_tanh(x):
    return 0.5 * x * (1.0 + jnp.tanh(math.sqrt(2.0 / math.pi) * (x + 0.044715 * (x * x * x))))


def _s5_kernel(u_ref, h0_ref, lam_ref, bmat_ref, cmat_ref, d_ref, wa_ref, wb_ref,
               y_ref, hlast_ref, hs_sc, h_sc, *, tc):
    ns = S5_GROUPS * S5_STATE
    rows = tc * SUBLANES

    @pl.when(pl.program_id(1) == 0)
    def _():
        h_sc[...] = h0_ref[...]

    u2 = u_ref[...].reshape(rows, 256)
    hs_sc[...] = _dot(u2.astype(BF16), bmat_ref[...]).reshape(tc, SUBLANES, 2 * ns)
    lr = jnp.broadcast_to(lam_ref[0:1, :], (SUBLANES, ns))
    li = jnp.broadcast_to(lam_ref[1:2, :], (SUBLANES, ns))

    def step(t, carry):
        hr, hi = carry
        b = hs_sc[t]
        nr = lr * hr - li * hi + b[:, 0:ns]
        ni = lr * hi + li * hr + b[:, ns:2 * ns]
        hs_sc[t] = jnp.concatenate([nr, ni], axis=-1)
        return nr, ni

    hr, hi = lax.fori_loop(0, tc, step, (h_sc[:, 0:ns], h_sc[:, ns:2 * ns]))
    hcat = jnp.concatenate([hr, hi], axis=-1)
    h_sc[...] = hcat
    hlast_ref[...] = hcat

    hs = hs_sc[...].reshape(rows, 2 * ns).astype(BF16)
    y = _dot(hs, cmat_ref[...]) + d_ref[...] * u2
    yb = _gelu_tanh(y).astype(BF16)
    out = _dot(yb, wa_ref[...]) * _sigmoid(_dot(yb, wb_ref[...]))
    y_ref[...] = out.reshape(tc, SUBLANES, 256)


def _s5(u_tm, h0, lw, tc):
    t, nbatch, _ = u_tm.shape
    ns2 = 2 * S5_GROUPS * S5_STATE
    in_specs = [
        pl.BlockSpec((tc, SUBLANES, 256), lambda b, c: (c, b, 0)),
        pl.BlockSpec((SUBLANES, ns2), lambda b, c: (b, 0)),
        _const_spec((2, ns2 // 2)), _const_spec((256, ns2)), _const_spec((ns2, 256)),
        _const_spec((1, 256)), _const_spec((256, 256)), _const_spec((256, 256)),
    ]
    out_specs = [pl.BlockSpec((tc, SUBLANES, 256), lambda b, c: (c, b, 0)),
                 pl.BlockSpec((SUBLANES, ns2), lambda b, c: (b, 0))]
    return pl.pallas_call(
        functools.partial(_s5_kernel, tc=tc),
        grid=(nbatch // SUBLANES, t // tc),
        in_specs=in_specs, out_specs=out_specs,
        out_shape=[jax.ShapeDtypeStruct((t, nbatch, 256), F32),
                   jax.ShapeDtypeStruct((nbatch, ns2), F32)],
        scratch_shapes=[pltpu.VMEM((tc, SUBLANES, ns2), F32), pltpu.VMEM((SUBLANES, ns2), F32)],
        compiler_params=_params(("parallel", "arbitrary")),
        name="s5",
    )(u_tm, h0, lw["s5_lam"], lw["s5_bmat"], lw["s5_cmat"], lw["d_s5"], lw["w_glu_a"],
      lw["w_glu_b"])


def _mla_prompt_kernel(q_ref, k_ref, wuv_ref, o_ref, m_sc, l_sc, acc_sc, *, tq):
    qi = pl.program_id(1)
    rows = MLA_HEADS * tq
    q = q_ref[...].reshape(rows, 256)
    m_sc[...] = jnp.full(m_sc.shape, -jnp.inf, F32)
    l_sc[...] = jnp.zeros(l_sc.shape, F32)
    acc_sc[...] = jnp.zeros(acc_sc.shape, F32)

    def block(kb, masked):
        k = k_ref[pl.ds(pl.multiple_of(kb * tq, tq), tq), :]
        s = _dot_nt(q, k)
        if masked:
            t_q = lax.broadcasted_iota(jnp.int32, (rows, tq), 0) % tq
            t_k = lax.broadcasted_iota(jnp.int32, (rows, tq), 1)
            s = jnp.where(t_k <= t_q, s, -jnp.inf)
        m_prev = m_sc[...]
        m_new = jnp.maximum(m_prev, jnp.max(s, axis=-1, keepdims=True))
        alpha = jnp.exp(m_prev - m_new)
        p = jnp.exp(s - _tile_lanes(m_new, tq))
        l_sc[...] = alpha * l_sc[...] + jnp.sum(p, axis=-1, keepdims=True)
        acc_sc[...] = alpha * acc_sc[...] + _dot(p.astype(BF16), k[:, 0:MLA_KV_LORA])
        m_sc[...] = m_new

    def body(kb, c):
        block(kb, False)
        return c

    lax.fori_loop(0, qi, body, 0)
    block(qi, True)
    o = (acc_sc[...] / l_sc[...]).astype(BF16)
    ocat = jnp.concatenate([o[h * tq:(h + 1) * tq] for h in range(MLA_HEADS)], axis=-1)
    o_ref[...] = _dot(ocat, wuv_ref[...])


def _mla_prompt(qcat, kcat, wuv, nbatch, t, tq):
    nq = t // tq
    return pl.pallas_call(
        functools.partial(_mla_prompt_kernel, tq=tq),
        grid=(nbatch, nq),
        in_specs=[pl.BlockSpec((MLA_HEADS, tq, 256), lambda b, i: (0, b * nq + i, 0)),
                  pl.BlockSpec((t, 256), lambda b, i: (b, 0)),
                  _const_spec(wuv.shape)],
        out_specs=pl.BlockSpec((tq, 256), lambda b, i: (b * nq + i, 0)),
        out_shape=jax.ShapeDtypeStruct((nbatch * t, 256), F32),
        scratch_shapes=[pltpu.VMEM((MLA_HEADS * tq, LANES), F32)] * 3,
        compiler_params=_params(("parallel", "arbitrary")),
        name="mla_prompt",
    )(qcat, kcat, wuv)


def _log_sigmoid_neg(z):
    return -_softplus(z)


def _head_stack(q):
    lane_head = lax.broadcasted_iota(jnp.int32, q.shape, 1) // SB_HEAD_DIM
    zero = jnp.zeros_like(q)
    return jnp.concatenate([jnp.where(lane_head == h, q, zero) for h in range(SB_HEADS)], axis=0)


def _head_unstack(acc, r):
    lane_head = lax.broadcasted_iota(jnp.int32, (r, 256), 1) // SB_HEAD_DIM
    out = jnp.zeros((r, 256), F32)
    for h in range(SB_HEADS):
        out = jnp.where(lane_head == h, acc[h * r:(h + 1) * r], out)
    return out


def _sb_weights(z, strict, carry, su):
    lk = _log_sigmoid_neg(z)
    if strict is not None:
        lk = jnp.where(strict, lk, 0.0)
    after = _dot3(lk, su) + (carry if carry.shape[1] == 1 else _tile_lanes(carry, z.shape[1]))
    w = jnp.exp(z + lk + after)
    if strict is not None:
        w = jnp.where(strict, w, 0.0)
    return w.astype(BF16), carry + jnp.sum(lk, axis=-1, keepdims=True)


def _sb_prompt_kernel(q_ref, k_ref, v_ref, o_ref, acc_sc, carry_sc, *, tq):
    qi = pl.program_id(1)
    rows = SB_HEADS * tq
    qs = _head_stack(q_ref[...])
    su = _tri(tq, "row_gt_col")
    acc_sc[...] = jnp.zeros(acc_sc.shape, F32)
    carry_sc[...] = jnp.zeros(carry_sc.shape, F32)

    def block(kb, masked):
        start = pl.multiple_of(kb * tq, tq)
        k = k_ref[pl.ds(start, tq), :]
        v = v_ref[pl.ds(start, tq), :]
        z = _dot_nt(qs, k)
        strict = None
        if masked:
            t_q = lax.broadcasted_iota(jnp.int32, (rows, tq), 0) % tq
            t_k = lax.broadcasted_iota(jnp.int32, (rows, tq), 1)
            strict = t_k < t_q
        w, carry = _sb_weights(z, strict, carry_sc[...], su)
        acc_sc[...] = acc_sc[...] + _dot(w, v)
        carry_sc[...] = carry

    block(qi, True)

    def body(j, c):
        block(qi - 1 - j, False)
        return c

    lax.fori_loop(0, qi, body, 0)
    o_ref[...] = _head_unstack(acc_sc[...], tq)


def _sb_prompt(sq, sk, sv, nbatch, t, tq):
    nq = t // tq
    return pl.pallas_call(
        functools.partial(_sb_prompt_kernel, tq=tq),
        grid=(nbatch, nq),
        in_specs=[pl.BlockSpec((tq, 256), lambda b, i: (b * nq + i, 0)),
                  pl.BlockSpec((t, 256), lambda b, i: (b, 0)),
                  pl.BlockSpec((t, 256), lambda b, i: (b, 0))],
        out_specs=pl.BlockSpec((tq, 256), lambda b, i: (b * nq + i, 0)),
        out_shape=jax.ShapeDtypeStruct((nbatch * t, 256), F32),
        scratch_shapes=[pltpu.VMEM((SB_HEADS * tq, 256), F32),
                        pltpu.VMEM((SB_HEADS * tq, LANES), F32)],
        compiler_params=_params(("parallel", "arbitrary")),
        name="sb_prompt",
    )(sq, sk, sv)


def _decode_kernel(pt_ref, qcat_ref, kcat_ref, sq_ref, sk_ref, sv_ref, wuv_ref,
                   ckv_hbm, kr_hbm, sbk_hbm, sbv_hbm, ymla_ref, ysb_ref,
                   ckv_buf, kr_buf, sbk_buf, sbv_buf, sems, m_sc, l_sc, acc_sc, sbacc_sc, carry_sc,
                   *, layer, ppc, ts, page, n_pages):
    b = pl.program_id(0)
    nb = pl.num_programs(0)
    nchunk = n_pages // ppc
    rows = MLA_HEADS * ts
    su = _tri(page, "row_gt_col")

    def page_copies(pages, slot):
        cps = []
        for j, pg in enumerate(pages):
            cps.append(pltpu.make_async_copy(
                ckv_hbm.at[layer, pg], ckv_buf.at[slot, pl.ds(j * page, page), :], sems.at[0, slot]))
            cps.append(pltpu.make_async_copy(kr_hbm.at[layer, pg], kr_buf.at[slot, j], sems.at[1, slot]))
            cps.append(pltpu.make_async_copy(sbk_hbm.at[layer, pg], sbk_buf.at[slot, j], sems.at[2, slot]))
            cps.append(pltpu.make_async_copy(sbv_hbm.at[layer, pg], sbv_buf.at[slot, j], sems.at[3, slot]))
        return cps

    def start_chunk(bb, c, slot):
        base = bb * n_pages + (n_pages - (c + 1) * ppc)
        for cp in page_copies([pt_ref[base + j] for j in range(ppc)], slot):
            cp.start()

    def wait_chunk(slot):
        for cp in page_copies([0] * ppc, slot):
            cp.wait()

    @pl.when(b == 0)
    def _():
        start_chunk(0, 0, 0)

    q = qcat_ref[...].reshape(rows, 256)
    qlat = q[:, 0:MLA_KV_LORA].astype(BF16)
    qrope = q[:, MLA_KV_LORA:MLA_KV_LORA + MLA_ROPE].astype(BF16)
    qsb = _head_stack(sq_ref[...]).astype(BF16)

    def mla_update(s, v):
        m_prev = m_sc[...]
        m_new = jnp.maximum(m_prev, jnp.max(s, axis=-1, keepdims=True))
        alpha = jnp.exp(m_prev - m_new)
        p = jnp.exp(s - _tile_lanes(m_new, s.shape[1]))
        l_sc[...] = alpha * l_sc[...] + jnp.sum(p, axis=-1, keepdims=True)
        acc_sc[...] = alpha * acc_sc[...] + _dot(p.astype(BF16), v)
        m_sc[...] = m_new

    m_sc[...] = jnp.full(m_sc.shape, -jnp.inf, F32)
    l_sc[...] = jnp.zeros(l_sc.shape, F32)
    acc_sc[...] = jnp.zeros(acc_sc.shape, F32)
    pad = lambda a: jnp.concatenate([a, jnp.zeros((page - ts, a.shape[1]), a.dtype)], axis=0)
    kc = pad(kcat_ref[...])
    t_q = lax.broadcasted_iota(jnp.int32, (rows, page), 0) % ts
    t_k = lax.broadcasted_iota(jnp.int32, (rows, page), 1)
    ckv_new = kc[:, 0:MLA_KV_LORA].astype(BF16)
    kr_new = kc[:, MLA_KV_LORA:MLA_KV_LORA + MLA_ROPE].astype(BF16)
    s_new = _dot_nt(qlat, ckv_new) + _dot_nt(qrope, kr_new)
    mla_update(jnp.where(t_k <= t_q, s_new, -jnp.inf), ckv_new)
    z_new = _dot_nt(qsb, pad(sk_ref[...]).astype(BF16))
    w_new, carry0 = _sb_weights(z_new, t_k < t_q, jnp.zeros((rows, 1), F32), su)
    sbacc_sc[...] = _dot(w_new, pad(sv_ref[...]).astype(BF16))
    carry_sc[...] = jnp.broadcast_to(carry0, carry_sc.shape)

    def compute(slot):
        lanes = lambda buf: jnp.concatenate([buf[slot, j] for j in range(ppc)], axis=-1)
        ckv = ckv_buf[slot].astype(BF16)
        s = _dot_nt(qlat, ckv) + _dot(qrope, lanes(kr_buf).astype(BF16))
        mla_update(s, ckv)

        z = _dot(qsb, lanes(sbk_buf).astype(BF16))
        lk = _log_sigmoid_neg(z)
        lk_rows = jnp.concatenate([lk[:, j * page:(j + 1) * page] for j in range(ppc)], axis=0)
        ain = _dot3(lk_rows, su)
        tot = ain[:, 0:1] + lk_rows[:, 0:1]
        carry = carry_sc[...]
        ws = [None] * ppc
        for j in reversed(range(ppc)):
            r = slice(j * rows, (j + 1) * rows)
            ws[j] = jnp.exp(z[:, j * page:(j + 1) * page] + lk_rows[r] + ain[r] + carry).astype(BF16)
            carry = carry + tot[r]
        w = jnp.concatenate(ws, axis=-1)
        sbacc_sc[...] = sbacc_sc[...] + _dot_nt(w, lanes(sbv_buf).astype(BF16))
        carry_sc[...] = carry

    def pair(i, carry):
        c0 = 2 * i
        start_chunk(b, c0 + 1, 1)
        wait_chunk(0)
        compute(0)
        wrap = c0 + 2 == nchunk
        b_next = jnp.where(wrap, jnp.where(b + 1 == nb, 0, b + 1), b)
        c_next = jnp.where(wrap, 0, c0 + 2)
        start_chunk(b_next, c_next, 0)
        wait_chunk(1)
        compute(1)
        return carry

    lax.fori_loop(0, nchunk // 2, pair, 0)

    o = (acc_sc[...] / l_sc[...]).astype(BF16)
    ocat = jnp.concatenate([o[h * ts:(h + 1) * ts] for h in range(MLA_HEADS)], axis=-1)
    ymla_ref[...] = _dot(ocat, wuv_ref[...])
    ysb_ref[...] = _head_unstack(sbacc_sc[...], ts)

    @pl.when(b == nb - 1)
    def _():
        wait_chunk(0)


def _decode(layer, page_table, qcat, kcat, sq, sk, sv, wuv, caches, ppc):
    cache_ckv, cache_kr, cache_sbk, cache_sbv = caches
    nbatch, n_pages = page_table.shape
    page = cache_ckv.shape[2]
    assert cache_kr.shape[2:] == (MLA_ROPE, page) and cache_sbk.shape[2:] == (GROUP_WIDTH, page)
    assert (n_pages // ppc) % 2 == 0
    ts = qcat.shape[1] // nbatch
    rows = MLA_HEADS * ts
    seq = lambda b, pt: (b, 0)
    in_specs = [
        pl.BlockSpec((MLA_HEADS, ts, 256), lambda b, pt: (0, b, 0)),
        pl.BlockSpec((ts, 256), seq), pl.BlockSpec((ts, 256), seq),
        pl.BlockSpec((ts, 256), seq), pl.BlockSpec((ts, 256), seq),
        pl.BlockSpec(wuv.shape, lambda b, pt: (0, 0)),
    ] + [pl.BlockSpec(memory_space=pl.ANY)] * 4
    grid_spec = pltpu.PrefetchScalarGridSpec(
        num_scalar_prefetch=1,
        grid=(nbatch,),
        in_specs=in_specs,
        out_specs=[pl.BlockSpec((ts, 256), seq)] * 2,
        scratch_shapes=[
            pltpu.VMEM((2, ppc * page, MLA_KV_LORA), F32),
            pltpu.VMEM((2, ppc, MLA_ROPE, page), F32),
            pltpu.VMEM((2, ppc, GROUP_WIDTH, page), F32),
            pltpu.VMEM((2, ppc, GROUP_WIDTH, page), F32),
            pltpu.SemaphoreType.DMA((4, 2)),
            pltpu.VMEM((rows, LANES), F32), pltpu.VMEM((rows, LANES), F32),
            pltpu.VMEM((rows, MLA_KV_LORA), F32), pltpu.VMEM((rows, 256), F32),
            pltpu.VMEM((rows, LANES), F32),
        ],
    )
    return pl.pallas_call(
        functools.partial(_decode_kernel, layer=layer, ppc=ppc, ts=ts, page=page, n_pages=n_pages),
        grid_spec=grid_spec,
        out_shape=[jax.ShapeDtypeStruct((nbatch * ts, 256), F32)] * 2,
        compiler_params=_params(("arbitrary",)),
        name="decode",
    )(page_table.reshape(-1), qcat, kcat, sq, sk, sv, wuv, *caches)


def _pad_lanes(w, width=LANES):
    return jnp.pad(w, ((0, 0), (0, width - w.shape[1])))


def _rot_cols(w):
    half = w.shape[-1] // 2
    return jnp.concatenate([-w[..., half:], w[..., :half]], axis=-1)


def _prep_layer(i, p):
    gw = GROUP_WIDTH
    conv_dim = gw + 2 * SSD_GROUPS * SSD_STATE
    sizes = (gw, conv_dim, SSD_HEADS, 256, MLA_KV_LORA, MLA_ROPE, gw, gw, gw, gw)
    offs = [0]
    for s in sizes:
        offs.append(offs[-1] + s)
    w_in = p["w_in"][i]
    z, xbc, dt, cq, ckv, kr, u, sq, sk, sv = [w_in[:, offs[j]:offs[j + 1]] for j in range(10)]
    w_in_ext = jnp.concatenate(
        [z, xbc, cq, ckv, u, sq, sk, sv, _pad_lanes(kr), _pad_lanes(_rot_cols(kr)), _pad_lanes(dt)],
        axis=1).astype(BF16)

    w_uq = p["w_uq"][i].reshape(256, MLA_HEADS, MLA_NOPE + MLA_ROPE)
    nope = w_uq[:, :, :MLA_NOPE].reshape(256, MLA_HEADS * MLA_NOPE)
    rope = w_uq[:, :, MLA_NOPE:]
    padh = lambda w: jnp.pad(w, ((0, 0), (0, 0), (0, LANES - MLA_ROPE))).reshape(256, MLA_HEADS * LANES)
    w_uq_ext = jnp.concatenate([nope, padh(rope), padh(_rot_cols(rope))], axis=1).astype(BF16)
    eye_h = jnp.eye(MLA_HEADS, dtype=F32)
    w_uk_bd = jnp.einsum("chd,hg->hdgc", p["w_uk"][i], eye_h).reshape(
        MLA_HEADS * MLA_NOPE, MLA_HEADS * MLA_KV_LORA).astype(BF16)
    w_uv_bd = jnp.einsum("chv,hg->hcgv", p["w_uv"][i], eye_h).reshape(
        MLA_HEADS * MLA_KV_LORA, GROUP_WIDTH).astype(BF16)

    lr, li = p["lam_re"][i], p["lam_im"][i]
    step = jnp.exp(p["log_step"][i])[:, None]
    mag = jnp.exp(lr * step)
    br, bi = mag * jnp.cos(li * step), mag * jnp.sin(li * step)
    den = lr * lr + li * li
    cr = ((br - 1.0) * lr + bi * li) / den
    ci = (bi * lr - (br - 1.0) * li) / den
    b_re, b_im = p["b_re"][i], p["b_im"][i]
    bbar_re = cr[..., None] * b_re - ci[..., None] * b_im
    bbar_im = cr[..., None] * b_im + ci[..., None] * b_re
    eye_g = jnp.eye(S5_GROUPS, dtype=F32)
    ns = S5_GROUPS * S5_STATE
    to_b = lambda b: jnp.einsum("gpc,gh->gchp", b, eye_g).reshape(S5_GROUPS * S5_GROUP, ns)
    s5_bmat = jnp.concatenate([to_b(bbar_re), to_b(bbar_im)], axis=1).astype(BF16)
    to_c = lambda c: jnp.einsum("gcp,gh->hpgc", c, eye_g).reshape(ns, S5_GROUPS * S5_GROUP)
    s5_cmat = jnp.concatenate([to_c(p["c_re"][i]), -to_c(p["c_im"][i])], axis=0).astype(BF16)
    s5_lam = jnp.stack([br.reshape(ns), bi.reshape(ns)])

    row = lambda v: v.reshape(1, -1)
    return {
        "g_mix": row(p["g_mix"][i]), "w_in_ext": w_in_ext, "g_q": row(p["g_q"][i]),
        "w_uq_ext": w_uq_ext, "w_uk_bd": w_uk_bd, "w_uv_bd": w_uv_bd, "g_kv": row(p["g_kv"][i]),
        "conv_w": p["conv_w"][i], "conv_b": row(p["conv_b"][i]),
        "dt_bias": _pad_lanes(row(p["dt_bias"][i])), "a_log": _pad_lanes(row(p["a_log"][i])),
        "d_exp": row(jnp.repeat(p["d_ssd"][i], SSD_HEAD_DIM)), "g_ssd": row(p["g_ssd"][i]),
        "s5_lam": s5_lam, "s5_bmat": s5_bmat, "s5_cmat": s5_cmat, "d_s5": row(p["d_s5"][i]),
        "w_glu_a": p["w_glu_a"][i].astype(BF16), "w_glu_b": p["w_glu_b"][i].astype(BF16),
        "w_out": p["w_out"][i].astype(BF16), "g_ffn": row(p["g_ffn"][i]),
        "w_gate": p["w_gate"][i].astype(BF16), "w_up": p["w_up"][i].astype(BF16),
        "w_down": p["w_down"][i].astype(BF16), "g_ple": row(p["g_ple"][i]),
        "w_pg": p["w_pg"][i].astype(BF16), "w_ple": p["w_ple"][i].astype(BF16),
    }


def _rope_table(pos):
    inv = ROPE_THETA ** (-jnp.arange(0, MLA_ROPE, 2, dtype=F32) / MLA_ROPE)
    ang = pos.astype(F32)[:, None] * inv[None, :]
    cos, sin = jnp.cos(ang), jnp.sin(ang)
    tile = lambda a: jnp.tile(jnp.concatenate([a, a], axis=-1), (1, LANES // MLA_ROPE))
    return jnp.concatenate([tile(cos), tile(sin)], axis=-1)


def _s5_state_in(h):
    b = h.shape[0]
    return jnp.moveaxis(h.reshape(b, S5_GROUPS * S5_STATE, 2), 2, 1).reshape(b, -1)


def _s5_state_out(h):
    b = h.shape[0]
    return jnp.moveaxis(h.reshape(b, 2, S5_GROUPS * S5_STATE), 1, 2).reshape(
        b, S5_GROUPS, S5_STATE, 2)


def _group_step(x, pemb, tab, lw, g_final, *, nseq, seq_len, conv_prev, ssd_h0, s5_h0,
                attend, tm, ssd_nb, s5_tc):
    (z, xbc, dt, ckv, kr, kcat, qcat, u, sqb, skb, svb, sk, sv) = _inproj(x, tab, lw, tm)
    hp = SSD_HEADS * SSD_HEAD_DIM
    prev8 = jnp.pad(conv_prev, ((0, 0), (SUBLANES - (SSD_CONV - 1), 0), (0, 0)))
    y_ssd, tail, ssd_h = _ssd(z, xbc, dt, prev8, ssd_h0.reshape(nseq, hp, SSD_STATE), lw,
                              nseq, seq_len, ssd_nb)
    u_tm = jnp.swapaxes(u.reshape(nseq, seq_len, 256), 0, 1)
    y_s5_tm, s5_h = _s5(u_tm, _s5_state_in(s5_h0), lw, s5_tc)
    y_s5 = jnp.swapaxes(y_s5_tm, 0, 1).reshape(nseq * seq_len, 256)
    y_mla, y_sb = attend(qcat, kcat, sqb, skb, svb, sk, sv)
    x_new, y_fin = _post(x, (y_ssd, y_mla, y_s5, y_sb), pemb, lw, g_final, tm)
    states = (
        ckv.reshape(nseq, seq_len, MLA_KV_LORA), kr.reshape(nseq, seq_len, MLA_ROPE),
        sk.reshape(nseq, seq_len, SB_HEADS, SB_HEAD_DIM),
        sv.reshape(nseq, seq_len, SB_HEADS, SB_HEAD_DIM),
        ssd_h.reshape(nseq, SSD_HEADS, SSD_HEAD_DIM, SSD_STATE),
        tail[:, SUBLANES - (SSD_CONV - 1):, :],
        _s5_state_out(s5_h),
    )
    return x_new, y_fin, states


def kernel(x_prompt, x_sample, p_prompt, p_sample, cache_mla_ckv, cache_mla_krope, cache_sb_k, cache_sb_v, state_ssd, state_ssd_conv, state_s5, page_table, g_mix, w_in, conv_w, conv_b, dt_bias, a_log, d_ssd, g_ssd, g_q, w_uq, g_kv, w_uk, w_uv, lam_re, lam_im, b_re, b_im, c_re, c_im, d_s5, log_step, w_glu_a, w_glu_b, w_out, g_ffn, w_gate, w_up, w_down, g_ple, w_pg, w_ple, g_final):
    params = dict(g_mix=g_mix, w_in=w_in, conv_w=conv_w, conv_b=conv_b, dt_bias=dt_bias,
                  a_log=a_log, d_ssd=d_ssd, g_ssd=g_ssd, g_q=g_q, w_uq=w_uq, g_kv=g_kv,
                  w_uk=w_uk, w_uv=w_uv, lam_re=lam_re, lam_im=lam_im, b_re=b_re, b_im=b_im,
                  c_re=c_re, c_im=c_im, d_s5=d_s5, log_step=log_step, w_glu_a=w_glu_a,
                  w_glu_b=w_glu_b, w_out=w_out, g_ffn=g_ffn, w_gate=w_gate, w_up=w_up,
                  w_down=w_down, g_ple=g_ple, w_pg=w_pg, w_ple=w_ple)
    bp, tp, d = x_prompt.shape
    bs, ts, _ = x_sample.shape
    depth = w_in.shape[0]
    n_pages = page_table.shape[1]
    page = cache_mla_ckv.shape[2]
    past_len = n_pages * page
    n_p, n_s = bp * tp, bs * ts
    assert ts == SUBLANES and tp % SSD_CHUNK == 0 and bp % SUBLANES == 0 and bs % SUBLANES == 0

    tm_p = _pick(tp, (256, 128))
    tm_s = _pick(n_s, (256, 128, 64))
    tq = _pick(tp, (256, 128))
    ppc = _pick(n_pages, (64, 32, 16, 8, 4, 2)) // 2
    ssd_nb_s = _pick(bs, (16, 8))
    s5_tc_p = _pick(tp, (64, 32))

    tab_p = _rope_table(jnp.arange(tp, dtype=jnp.int32))
    tab_s = jnp.tile(_rope_table(past_len + jnp.arange(ts, dtype=jnp.int32)), (tm_s // ts, 1))
    g_fin = g_final.reshape(1, d)
    kv_t = lambda c: jnp.transpose(c, (0, 1, 3, 4, 2)).reshape(c.shape[:2] + (GROUP_WIDTH, page))
    caches = (cache_mla_ckv, jnp.transpose(cache_mla_krope, (0, 1, 3, 2)),
              kv_t(cache_sb_k), kv_t(cache_sb_v))

    xp = x_prompt.reshape(n_p, d)
    xs = x_sample.reshape(n_s, d)
    new_p, new_s = [], []
    yp = ys = None
    for i in range(depth):
        lw = _prep_layer(i, params)

        def attend_prompt(qcat, kcat, sqb, skb, svb, sk, sv, lw=lw):
            return (_mla_prompt(qcat, kcat, lw["w_uv_bd"], bp, tp, tq),
                    _sb_prompt(sqb, skb, svb, bp, tp, tq))

        def attend_sample(qcat, kcat, sqb, skb, svb, sk, sv, lw=lw, i=i):
            return _decode(i, page_table, qcat.astype(F32), kcat.astype(F32), sqb.astype(F32),
                           sk, sv, lw["w_uv_bd"], caches, ppc)

        xp, yp, st_p = _group_step(
            xp, p_prompt[i].reshape(n_p, -1), tab_p, lw, g_fin, nseq=bp, seq_len=tp,
            conv_prev=jnp.zeros((bp, SSD_CONV - 1, 768), F32),
            ssd_h0=jnp.zeros((bp, SSD_HEADS, SSD_HEAD_DIM, SSD_STATE), F32),
            s5_h0=jnp.zeros((bp, S5_GROUPS, S5_STATE, 2), F32),
            attend=attend_prompt, tm=tm_p, ssd_nb=1, s5_tc=s5_tc_p)
        xs, ys, st_s = _group_step(
            xs, p_sample[i].reshape(n_s, -1), tab_s, lw, g_fin, nseq=bs, seq_len=ts,
            conv_prev=state_ssd_conv[i], ssd_h0=state_ssd[i], s5_h0=state_s5[i],
            attend=attend_sample, tm=tm_s, ssd_nb=ssd_nb_s, s5_tc=ts)
        new_p.append(st_p)
        new_s.append(st_s)

    stack = lambda lst: tuple(jnp.stack([st[j] for st in lst]) for j in range(7))
    return ((yp.reshape(bp, tp, d), ys.reshape(bs, ts, d)) + stack(new_p) + stack(new_s))
```

```python
import functools
import math

import jax
import jax.numpy as jnp
from jax import lax
from jax.experimental import pallas as pl
from jax.experimental.pallas import tpu as pltpu

F32 = jnp.float32
BF16 = jnp.bfloat16
EPS = 1e-6
ROPE_THETA = 10000.0

LANES = 128
SUBLANES = 8
VMEM_LIMIT = 56 * 1024 * 1024

GROUP_WIDTH = 256
SSD_HEADS = 8
SSD_HEAD_DIM = 32
SSD_GROUPS = 2
SSD_STATE = 128
SSD_CONV = 4
SSD_CHUNK = 128
MLA_HEADS = 4
MLA_NOPE = 64
MLA_ROPE = 32
MLA_KV_LORA = 128
S5_GROUPS = 16
S5_GROUP = 16
S5_STATE = 64
SB_HEADS = 4
SB_HEAD_DIM = 64
NEG_BIG = -1e30


def _dot(a, b):
    return jnp.dot(a, b, preferred_element_type=F32)


def _dot_nt(a, b):
    return lax.dot_general(a, b, (((1,), (1,)), ((), ())), preferred_element_type=F32)


def _dot_tn(a, b):
    return lax.dot_general(a, b, (((0,), (0,)), ((), ())), preferred_element_type=F32)


def _split3(x):
    hi = x.astype(BF16)
    r = x - hi.astype(F32)
    mid = r.astype(BF16)
    lo = (r - mid.astype(F32)).astype(BF16)
    return hi, mid, lo


def _dot3(x, m):
    hi, mid, lo = _split3(x)
    return _dot(hi, m) + _dot(mid, m) + _dot(lo, m)


def _dot2(x, m):
    hi = x.astype(BF16)
    lo = (x - hi.astype(F32)).astype(BF16)
    return _dot(hi, m) + _dot(lo, m)


def _dot3_left(m, x):
    hi, mid, lo = _split3(x)
    return _dot(m, hi) + _dot(m, mid) + _dot(m, lo)


def _rms(x, g):
    return x * lax.rsqrt(jnp.mean(x * x, axis=-1, keepdims=True) + EPS) * g


def _sigmoid(x):
    return 1.0 / (1.0 + jnp.exp(-x))


def _softplus(x):
    return jnp.maximum(x, 0.0) + jnp.log(1.0 + jnp.exp(-jnp.abs(x)))


def _tri(n, kind):
    r = lax.broadcasted_iota(jnp.int32, (n, n), 0)
    c = lax.broadcasted_iota(jnp.int32, (n, n), 1)
    cond = {"lower_incl": c <= r, "row_gt_col": r > c}[kind]
    return jnp.where(cond, 1.0, 0.0).astype(BF16)


def _const_spec(shape):
    nd = len(shape)
    return pl.BlockSpec(shape, lambda *_: (0,) * nd)


def _tile_lanes(a, width):
    reps = width // a.shape[1]
    return a if reps == 1 else jnp.concatenate([a] * reps, axis=-1)


def _params(sem):
    return pltpu.CompilerParams(dimension_semantics=sem, vmem_limit_bytes=VMEM_LIMIT)


def _pick(n, cands):
    for c in cands:
        if n % c == 0:
            return c
    raise ValueError(f"no tile in {cands} divides {n}")


def _inproj_kernel(x_ref, tab_ref, gmix_ref, win_ref, gq_ref, wuq_ref, wuk_ref, gkv_ref,
                   z_ref, xbc_ref, dt_ref, ckv_ref, kr_ref, kcat_ref, qcat_ref, u_ref,
                   sqb_ref, skb_ref, svb_ref, sk_ref, sv_ref):
    x = x_ref[...]
    hb = _rms(x, gmix_ref[...]).astype(BF16)
    proj = _dot(hb, win_ref[...])
    z_ref[...] = proj[:, 0:256]
    xbc_ref[...] = proj[:, 256:1024]
    cq = proj[:, 1024:1280]
    ckv_raw = proj[:, 1280:1408]
    u_ref[...] = proj[:, 1408:1664]
    sq = proj[:, 1664:1920]
    sk = proj[:, 1920:2176]
    sv = proj[:, 2176:2432]
    kr_raw = proj[:, 2432:2560]
    kr_rot = proj[:, 2560:2688]
    dt_ref[...] = proj[:, 2688:2816]

    cos = tab_ref[:, 0:LANES]
    sin = tab_ref[:, LANES:2 * LANES]
    kr = kr_raw * cos + kr_rot * sin
    ckv = _rms(ckv_raw, gkv_ref[...])
    ckv_ref[...] = ckv
    kr_ref[...] = kr[:, 0:MLA_ROPE]
    kcat_ref[...] = jnp.concatenate([ckv, kr], axis=-1).astype(BF16)

    cqn = _rms(cq, gq_ref[...]).astype(BF16)
    q = _dot(cqn, wuq_ref[...])
    q_lat = _dot(q[:, 0:256].astype(BF16), wuk_ref[...])
    scale = 1.0 / math.sqrt(MLA_NOPE + MLA_ROPE)
    for h in range(MLA_HEADS):
        lo = 256 + h * LANES
        q_rope = q[:, lo:lo + LANES] * cos + q[:, lo + 512:lo + 512 + LANES] * sin
        qh = jnp.concatenate([q_lat[:, h * LANES:(h + 1) * LANES], q_rope], axis=-1)
        qcat_ref[h] = (qh * scale).astype(BF16)

    sqb_ref[...] = (sq * (1.0 / math.sqrt(SB_HEAD_DIM))).astype(BF16)
    skb_ref[...] = sk.astype(BF16)
    svb_ref[...] = sv.astype(BF16)
    sk_ref[...] = sk
    sv_ref[...] = sv


def _inproj(x, tab, lw, tm, time_major_len=None):
    n, d = x.shape
    nper = tab.shape[0] // tm
    row = lambda i: (i, 0)
    win, wuq, wuk = lw["w_in_ext"], lw["w_uq_ext"], lw["w_uk_bd"]
    in_specs = [
        pl.BlockSpec((tm, d), row),
        pl.BlockSpec((tm, 2 * LANES), lambda i: (i % nper, 0)),
        _const_spec((1, d)),
        _const_spec(win.shape),
        _const_spec((1, 256)),
        _const_spec(wuq.shape),
        _const_spec(wuk.shape),
        _const_spec((1, MLA_KV_LORA)),
    ]
    outs = [
        ((n, 256), F32), ((n, 768), F32), ((n, LANES), F32), ((n, MLA_KV_LORA), F32),
        ((n, MLA_ROPE), F32), ((n, 256), BF16), ((MLA_HEADS, n, 256), BF16), ((n, 256), F32),
        ((n, 256), BF16), ((n, 256), BF16), ((n, 256), BF16), ((n, 256), F32), ((n, 256), F32),
    ]
    out_specs = []
    for shp, _ in outs:
        if len(shp) == 3:
            out_specs.append(pl.BlockSpec((MLA_HEADS, tm, 256), lambda i: (0, i, 0)))
        else:
            out_specs.append(pl.BlockSpec((tm, shp[1]), row))
    if time_major_len is not None:
        nt = time_major_len // tm
        outs[7] = ((time_major_len, (n // time_major_len) * 256), F32)
        out_specs[7] = pl.BlockSpec((tm, 256), lambda i: (i % nt, i // nt))
    return pl.pallas_call(
        _inproj_kernel,
        grid=(n // tm,),
        in_specs=in_specs,
        out_specs=out_specs,
        out_shape=[jax.ShapeDtypeStruct(s, t) for s, t in outs],
        compiler_params=_params(("parallel",)),
        name="inproj",
    )(x, tab, lw["g_mix"], win, lw["g_q"], wuq, wuk, lw["g_kv"])


def _post_kernel(x_ref, yssd_ref, ymla_ref, ys5_ref, ysb_ref, p_ref, wout_ref, gffn_ref,
                 wg_ref, wu_ref, wd_ref, gple_ref, wpg_ref, wple_ref, gfin_ref,
                 o_ref, *, hidden_chunk, final):
    mix = jnp.concatenate([yssd_ref[...], ymla_ref[...], ys5_ref[...], ysb_ref[...]],
                          axis=-1).astype(BF16)
    x1 = x_ref[...] + _dot(mix, wout_ref[...])
    h2 = _rms(x1, gffn_ref[...]).astype(BF16)
    hidden = wg_ref.shape[1]
    acc = jnp.zeros_like(x1)
    for c in range(hidden // hidden_chunk):
        sl = slice(c * hidden_chunk, (c + 1) * hidden_chunk)
        g = _dot(h2, wg_ref[:, sl])
        up = _dot(h2, wu_ref[:, sl])
        a = (g * _sigmoid(g) * up).astype(BF16)
        acc = acc + _dot(a, wd_ref[sl, :])
    x2 = x1 + acc
    gate = _sigmoid(_dot(_rms(x2, gple_ref[...]).astype(BF16), wpg_ref[...]))
    x3 = x2 + _dot(p_ref[...].astype(BF16), wple_ref[...]) * gate
    o_ref[...] = _rms(x3, gfin_ref[...]) if final else x3


def _post(x, ys, pemb, lw, g_final, tm, final, time_major_len=None):
    n, d = x.shape
    row = lambda i: (i, 0)
    w1 = functools.partial(pl.BlockSpec, pipeline_mode=pl.Buffered(1))
    cmap = lambda i: (0, 0)
    ws = [lw["w_out"], lw["g_ffn"], lw["w_gate"], lw["w_up"], lw["w_down"], lw["g_ple"],
          lw["w_pg"], lw["w_ple"], g_final]
    in_specs = ([pl.BlockSpec((tm, d), row)] + [pl.BlockSpec((tm, 256), row)] * 4
                + [pl.BlockSpec((tm, pemb.shape[1]), row)] + [w1(w.shape, cmap) for w in ws])
    if time_major_len is not None:
        nt = time_major_len // tm
        in_specs[3] = pl.BlockSpec((tm, 256), lambda i: (i % nt, i // nt))
    return pl.pallas_call(
        functools.partial(_post_kernel, hidden_chunk=256, final=final),
        grid=(n // tm,),
        in_specs=in_specs,
        out_specs=pl.BlockSpec((tm, d), row),
        out_shape=jax.ShapeDtypeStruct((n, d), F32),
        compiler_params=_params(("parallel",)),
        name="post",
    )(x, *ys, pemb, *ws)


def _expand_heads(a, g):
    r = a.shape[0]
    lane_head = lax.broadcasted_iota(jnp.int32, (r, LANES), 1) // SSD_HEAD_DIM
    out = jnp.zeros((r, LANES), F32)
    per = SSD_HEADS // SSD_GROUPS
    for j in range(per):
        h = g * per + j
        out = jnp.where(lane_head == j, jnp.broadcast_to(a[:, h:h + 1], (r, LANES)), out)
    return out


def _ssd_kernel(z_ref, xbc_ref, dt_ref, prev_ref, h0_ref, cw_ref, cb_ref, dtb_ref, alog_ref,
                dexp_ref, g_ref, y_ref, tail_ref, hfin_ref, ht_sc, tail_sc, *, nb, seq_len):
    Q = SSD_CHUNK
    lv = min(seq_len, Q)
    nc = max(seq_len // Q, 1)
    per = SSD_HEADS // SSD_GROUPS
    tri_l = _tri(Q, "lower_incl")
    row_i = lax.broadcasted_iota(jnp.int32, (Q, Q), 0)
    col_i = lax.broadcasted_iota(jnp.int32, (Q, Q), 1)
    causal = col_i <= row_i
    lane_head = lax.broadcasted_iota(jnp.int32, (Q, LANES), 1) // SSD_HEAD_DIM
    sub8 = lax.broadcasted_iota(jnp.int32, (SUBLANES, 768), 0)
    neg_a = -jnp.exp(alog_ref[...])

    def pad_rows(a):
        if lv == Q:
            return a
        return jnp.concatenate([a, jnp.zeros((Q - lv, a.shape[1]), a.dtype)], axis=0)

    def chunk(base):
        raw = pad_rows(xbc_ref[pl.ds(base, lv), :])
        prev8 = tail_sc[...]
        conv = cb_ref[...] + raw * cw_ref[SSD_CONV - 1:SSD_CONV, :]
        for k in range(1, SSD_CONV):
            ra = pltpu.roll(raw, k, 0)
            rp = pltpu.roll(prev8, k, 0)
            top = jnp.where(sub8 < k, rp, ra[0:SUBLANES])
            shifted = top if Q == SUBLANES else jnp.concatenate([top, ra[SUBLANES:]], axis=0)
            conv = conv + shifted * cw_ref[SSD_CONV - 1 - k:SSD_CONV - k, :]
        tail_sc[...] = raw[lv - SUBLANES:lv]
        xbc = conv * _sigmoid(conv)

        dt = _softplus(pad_rows(dt_ref[pl.ds(base, lv), :]) + dtb_ref[...])
        if lv < Q:
            dt = jnp.where(lax.broadcasted_iota(jnp.int32, (Q, LANES), 0) < lv, dt, 0.0)
        a_cum = _dot3_left(tri_l, dt * neg_a)
        a_cum_t = a_cum.T

        ys = []
        for g in range(SSD_GROUPS):
            xs = xbc[:, g * LANES:(g + 1) * LANES]
            bm = xbc[:, 256 + g * LANES:256 + (g + 1) * LANES].astype(BF16)
            cm = xbc[:, 512 + g * LANES:512 + (g + 1) * LANES].astype(BF16)
            dt_e = _expand_heads(dt, g)
            col_e = _expand_heads(a_cum, g)
            last_e = _expand_heads(a_cum[Q - 1:Q, :], g)
            xdt = xs * dt_e
            cb = _dot_nt(cm, bm)
            y = jnp.zeros((Q, LANES), F32)
            for j in range(per):
                h = g * per + j
                seg = a_cum[:, h:h + 1] - a_cum_t[h:h + 1, :]
                lmat = jnp.exp(jnp.where(causal, seg, NEG_BIG))
                xh = jnp.where(lane_head == j, xdt, 0.0).astype(BF16)
                y = y + _dot((cb * lmat).astype(BF16), xh)
            ht = ht_sc[g]
            y = y + _dot(cm, ht.astype(BF16)) * jnp.exp(col_e)
            y = y + dexp_ref[:, g * LANES:(g + 1) * LANES] * xs
            xdec = (xdt * jnp.exp(last_e - col_e)).astype(BF16)
            ht_sc[g] = ht * jnp.exp(last_e) + _dot_tn(bm, xdec)
            ys.append(y)
        y = jnp.concatenate(ys, axis=-1)
        zc = pad_rows(z_ref[pl.ds(base, lv), :])
        y = _rms(y * (zc * _sigmoid(zc)), g_ref[...])
        y_ref[pl.ds(base, lv), :] = y[0:lv]

    def sequence(s, carry):
        tail_sc[...] = prev_ref[s]
        for g in range(SSD_GROUPS):
            ht_sc[g] = h0_ref[s, g * LANES:(g + 1) * LANES, :].T

        def body(c, cc):
            chunk(pl.multiple_of(s * seq_len + c * lv, SUBLANES))
            return cc

        lax.fori_loop(0, nc, body, 0)
        tail_ref[s] = tail_sc[...]
        for g in range(SSD_GROUPS):
            hfin_ref[s, g * LANES:(g + 1) * LANES, :] = ht_sc[g].T
        return carry

    lax.fori_loop(0, nb, sequence, 0)


def _ssd(z, xbc, dt, prev8, h0, lw, nseq, seq_len, nb):
    rows = nb * seq_len
    row = lambda i: (i, 0)
    seq3 = lambda i: (i, 0, 0)
    hp = SSD_HEADS * SSD_HEAD_DIM
    in_specs = [
        pl.BlockSpec((rows, 256), row), pl.BlockSpec((rows, 768), row),
        pl.BlockSpec((rows, LANES), row),
        pl.BlockSpec((nb, SUBLANES, 768), seq3), pl.BlockSpec((nb, hp, SSD_STATE), seq3),
        _const_spec((SSD_CONV, 768)), _const_spec((1, 768)), _const_spec((1, LANES)),
        _const_spec((1, LANES)), _const_spec((1, 256)), _const_spec((1, 256)),
    ]
    out_specs = [pl.BlockSpec((rows, 256), row), pl.BlockSpec((nb, SUBLANES, 768), seq3),
                 pl.BlockSpec((nb, hp, SSD_STATE), seq3)]
    out_shape = [jax.ShapeDtypeStruct((nseq * seq_len, 256), F32),
                 jax.ShapeDtypeStruct((nseq, SUBLANES, 768), F32),
                 jax.ShapeDtypeStruct((nseq, hp, SSD_STATE), F32)]
    return pl.pallas_call(
        functools.partial(_ssd_kernel, nb=nb, seq_len=seq_len),
        grid=(nseq // nb,),
        in_specs=in_specs, out_specs=out_specs, out_shape=out_shape,
        scratch_shapes=[pltpu.VMEM((SSD_GROUPS, SSD_STATE, LANES), F32),
                        pltpu.VMEM((SUBLANES, 768), F32)],
        compiler_params=_params(("parallel",)),
        name="ssd",
    )(z, xbc, dt, prev8, h0, lw["conv_w"], lw["conv_b"], lw["dt_bias"], lw["a_log"],
      lw["d_exp"], lw["g_ssd"])


def _gelu_tanh(x):
    return 0.5 * x * (1.0 + jnp.tanh(math.sqrt(2.0 / math.pi) * (x + 0.044715 * (x * x * x))))


def _s5_kernel(u_ref, h0_ref, lam_ref, bmat_ref, cmat_ref, d_ref, wa_ref, wb_ref,
               y_ref, hlast_ref, hs_sc, h_sc, *, tc):
    ns = S5_GROUPS * S5_STATE
    rows = tc * SUBLANES

    @pl.when(pl.program_id(1) == 0)
    def _():
        h_sc[...] = h0_ref[...]

    u2 = u_ref[...].reshape(rows, 256)
    hs_sc[...] = _dot(u2.astype(BF16), bmat_ref[...]).reshape(tc, SUBLANES, 2 * ns)
    lr = jnp.broadcast_to(lam_ref[0:1, :], (SUBLANES, ns))
    li = jnp.broadcast_to(lam_ref[1:2, :], (SUBLANES, ns))

    def step(t, carry):
        hr, hi = carry
        b = hs_sc[t]
        nr = lr * hr - li * hi + b[:, 0:ns]
        ni = lr * hi + li * hr + b[:, ns:2 * ns]
        hs_sc[t] = jnp.concatenate([nr, ni], axis=-1)
        return nr, ni

    hr, hi = lax.fori_loop(0, tc, step, (h_sc[:, 0:ns], h_sc[:, ns:2 * ns]))
    hcat = jnp.concatenate([hr, hi], axis=-1)
    h_sc[...] = hcat
    hlast_ref[...] = hcat

    hs = hs_sc[...].reshape(rows, 2 * ns).astype(BF16)
    y = _dot(hs, cmat_ref[...]) + d_ref[...] * u2
    yb = _gelu_tanh(y).astype(BF16)
    out = _dot(yb, wa_ref[...]) * _sigmoid(_dot(yb, wb_ref[...]))
    y_ref[...] = out.reshape(tc, SUBLANES, 256)


def _s5(u_tm, h0, lw, tc):
    t, nbatch, _ = u_tm.shape
    ns2 = 2 * S5_GROUPS * S5_STATE
    in_specs = [
        pl.BlockSpec((tc, SUBLANES, 256), lambda b, c: (c, b, 0)),
        pl.BlockSpec((SUBLANES, ns2), lambda b, c: (b, 0)),
        _const_spec((2, ns2 // 2)), _const_spec((256, ns2)), _const_spec((ns2, 256)),
        _const_spec((1, 256)), _const_spec((256, 256)), _const_spec((256, 256)),
    ]
    out_specs = [pl.BlockSpec((tc, SUBLANES, 256), lambda b, c: (c, b, 0)),
                 pl.BlockSpec((SUBLANES, ns2), lambda b, c: (b, 0))]
    return pl.pallas_call(
        functools.partial(_s5_kernel, tc=tc),
        grid=(nbatch // SUBLANES, t // tc),
        in_specs=in_specs, out_specs=out_specs,
        out_shape=[jax.ShapeDtypeStruct((t, nbatch, 256), F32),
                   jax.ShapeDtypeStruct((nbatch, ns2), F32)],
        scratch_shapes=[pltpu.VMEM((tc, SUBLANES, ns2), F32), pltpu.VMEM((SUBLANES, ns2), F32)],
        compiler_params=_params(("parallel", "arbitrary")),
        name="s5",
    )(u_tm, h0, lw["s5_lam"], lw["s5_bmat"], lw["s5_cmat"], lw["d_s5"], lw["w_glu_a"],
      lw["w_glu_b"])


def _mla_prompt_kernel(q_ref, k_ref, wuv_ref, o_ref, m_sc, l_sc, acc_sc, *, tq):
    qi = pl.program_id(1)
    rows = MLA_HEADS * tq
    q = q_ref[...].reshape(rows, 256)
    m_sc[...] = jnp.full(m_sc.shape, -jnp.inf, F32)
    l_sc[...] = jnp.zeros(l_sc.shape, F32)
    acc_sc[...] = jnp.zeros(acc_sc.shape, F32)

    def block(kb, masked):
        k = k_ref[pl.ds(pl.multiple_of(kb * tq, tq), tq), :]
        s = _dot_nt(q, k)
        if masked:
            t_q = lax.broadcasted_iota(jnp.int32, (rows, tq), 0) % tq
            t_k = lax.broadcasted_iota(jnp.int32, (rows, tq), 1)
            s = jnp.where(t_k <= t_q, s, -jnp.inf)
        m_prev = m_sc[...]
        m_new = jnp.maximum(m_prev, jnp.max(s, axis=-1, keepdims=True))
        alpha = jnp.exp(m_prev - m_new)
        p = jnp.exp(s - _tile_lanes(m_new, tq))
        l_sc[...] = alpha * l_sc[...] + jnp.sum(p, axis=-1, keepdims=True)
        acc_sc[...] = alpha * acc_sc[...] + _dot(p.astype(BF16), k[:, 0:MLA_KV_LORA])
        m_sc[...] = m_new

    def body(kb, c):
        block(kb, False)
        return c

    lax.fori_loop(0, qi, body, 0)
    block(qi, True)
    o = (acc_sc[...] / l_sc[...]).astype(BF16)
    ocat = jnp.concatenate([o[h * tq:(h + 1) * tq] for h in range(MLA_HEADS)], axis=-1)
    o_ref[...] = _dot(ocat, wuv_ref[...])


def _mla_prompt(qcat, kcat, wuv, nbatch, t, tq):
    nq = t // tq
    return pl.pallas_call(
        functools.partial(_mla_prompt_kernel, tq=tq),
        grid=(nbatch, nq),
        in_specs=[pl.BlockSpec((MLA_HEADS, tq, 256), lambda b, i: (0, b * nq + i, 0)),
                  pl.BlockSpec((t, 256), lambda b, i: (b, 0)),
                  _const_spec(wuv.shape)],
        out_specs=pl.BlockSpec((tq, 256), lambda b, i: (b * nq + i, 0)),
        out_shape=jax.ShapeDtypeStruct((nbatch * t, 256), F32),
        scratch_shapes=[pltpu.VMEM((MLA_HEADS * tq, LANES), F32)] * 3,
        compiler_params=_params(("parallel", "arbitrary")),
        name="mla_prompt",
    )(qcat, kcat, wuv)


def _log_sigmoid_neg(z):
    return -_softplus(z)


def _head_stack(q):
    lane_head = lax.broadcasted_iota(jnp.int32, q.shape, 1) // SB_HEAD_DIM
    zero = jnp.zeros_like(q)
    return jnp.concatenate([jnp.where(lane_head == h, q, zero) for h in range(SB_HEADS)], axis=0)


def _head_unstack(acc, r):
    lane_head = lax.broadcasted_iota(jnp.int32, (r, 256), 1) // SB_HEAD_DIM
    out = jnp.zeros((r, 256), F32)
    for h in range(SB_HEADS):
        out = jnp.where(lane_head == h, acc[h * r:(h + 1) * r], out)
    return out


def _sb_weights(z, strict, carry, su, tri_dot=_dot3):
    lk = _log_sigmoid_neg(z)
    if strict is not None:
        lk = jnp.where(strict, lk, 0.0)
    after = tri_dot(lk, su) + (carry if carry.shape[1] == 1 else _tile_lanes(carry, z.shape[1]))
    w = jnp.exp(z + lk + after)
    if strict is not None:
        w = jnp.where(strict, w, 0.0)
    return w.astype(BF16), carry + jnp.sum(lk, axis=-1, keepdims=True)


def _sb_prompt_kernel(q_ref, k_ref, v_ref, o_ref, acc_sc, carry_sc, *, tq):
    qi = pl.program_id(1)
    rows = SB_HEADS * tq
    qs = _head_stack(q_ref[...])
    su = _tri(tq, "row_gt_col")
    acc_sc[...] = jnp.zeros(acc_sc.shape, F32)
    carry_sc[...] = jnp.zeros(carry_sc.shape, F32)

    def block(kb, masked):
        start = pl.multiple_of(kb * tq, tq)
        k = k_ref[pl.ds(start, tq), :]
        v = v_ref[pl.ds(start, tq), :]
        z = _dot_nt(qs, k)
        strict = None
        if masked:
            t_q = lax.broadcasted_iota(jnp.int32, (rows, tq), 0) % tq
            t_k = lax.broadcasted_iota(jnp.int32, (rows, tq), 1)
            strict = t_k < t_q
        w, carry = _sb_weights(z, strict, carry_sc[...], su, tri_dot=_dot2)
        acc_sc[...] = acc_sc[...] + _dot(w, v)
        carry_sc[...] = carry

    block(qi, True)

    def body(j, c):
        block(qi - 1 - j, False)
        return c

    lax.fori_loop(0, qi, body, 0)
    o_ref[...] = _head_unstack(acc_sc[...], tq)


def _sb_prompt(sq, sk, sv, nbatch, t, tq):
    nq = t // tq
    return pl.pallas_call(
        functools.partial(_sb_prompt_kernel, tq=tq),
        grid=(nbatch, nq),
        in_specs=[pl.BlockSpec((tq, 256), lambda b, i: (b * nq + i, 0)),
                  pl.BlockSpec((t, 256), lambda b, i: (b, 0)),
                  pl.BlockSpec((t, 256), lambda b, i: (b, 0))],
        out_specs=pl.BlockSpec((tq, 256), lambda b, i: (b * nq + i, 0)),
        out_shape=jax.ShapeDtypeStruct((nbatch * t, 256), F32),
        scratch_shapes=[pltpu.VMEM((SB_HEADS * tq, 256), F32),
                        pltpu.VMEM((SB_HEADS * tq, LANES), F32)],
        compiler_params=_params(("parallel", "arbitrary")),
        name="sb_prompt",
    )(sq, sk, sv)


def _decode_kernel(pt_ref, qcat_ref, kcat_ref, sq_ref, sk_ref, sv_ref, wuv_ref,
                   ckv_hbm, kr_hbm, sbk_hbm, sbv_hbm, ymla_ref, ysb_ref,
                   ckv_buf, kr_buf, sbk_buf, sbv_buf, sems, m_sc, l_sc, acc_sc, sbacc_sc, carry_sc,
                   *, layer, ppc, ts, page, n_pages):
    b = pl.program_id(0)
    nb = pl.num_programs(0)
    nchunk = n_pages // ppc
    rows = MLA_HEADS * ts
    su = _tri(page, "row_gt_col")

    def page_copies(pages, slot):
        cps = []
        for j, pg in enumerate(pages):
            cps.append(pltpu.make_async_copy(
                ckv_hbm.at[layer, pg], ckv_buf.at[slot, pl.ds(j * page, page), :], sems.at[0, slot]))
            cps.append(pltpu.make_async_copy(kr_hbm.at[layer, pg], kr_buf.at[slot, j], sems.at[1, slot]))
            cps.append(pltpu.make_async_copy(sbk_hbm.at[layer, pg], sbk_buf.at[slot, j], sems.at[2, slot]))
            cps.append(pltpu.make_async_copy(sbv_hbm.at[layer, pg], sbv_buf.at[slot, j], sems.at[3, slot]))
        return cps

    def start_chunk(bb, c, slot):
        base = bb * n_pages + (n_pages - (c + 1) * ppc)
        for cp in page_copies([pt_ref[base + j] for j in range(ppc)], slot):
            cp.start()

    def wait_chunk(slot):
        for cp in page_copies([0] * ppc, slot):
            cp.wait()

    @pl.when(b == 0)
    def _():
        start_chunk(0, 0, 0)

    q = qcat_ref[...].reshape(rows, 256)
    qlat = q[:, 0:MLA_KV_LORA].astype(BF16)
    qrope = q[:, MLA_KV_LORA:MLA_KV_LORA + MLA_ROPE].astype(BF16)
    qsb = _head_stack(sq_ref[...]).astype(BF16)

    def mla_update(s, v):
        m_prev = m_sc[...]
        m_new = jnp.maximum(m_prev, jnp.max(s, axis=-1, keepdims=True))
        alpha = jnp.exp(m_prev - m_new)
        p = jnp.exp(s - _tile_lanes(m_new, s.shape[1]))
        l_sc[...] = alpha * l_sc[...] + jnp.sum(p, axis=-1, keepdims=True)
        acc_sc[...] = alpha * acc_sc[...] + _dot(p.astype(BF16), v)
        m_sc[...] = m_new

    m_sc[...] = jnp.full(m_sc.shape, -jnp.inf, F32)
    l_sc[...] = jnp.zeros(l_sc.shape, F32)
    acc_sc[...] = jnp.zeros(acc_sc.shape, F32)
    pad = lambda a: jnp.concatenate([a, jnp.zeros((page - ts, a.shape[1]), a.dtype)], axis=0)
    kc = pad(kcat_ref[...])
    t_q = lax.broadcasted_iota(jnp.int32, (rows, page), 0) % ts
    t_k = lax.broadcasted_iota(jnp.int32, (rows, page), 1)
    ckv_new = kc[:, 0:MLA_KV_LORA].astype(BF16)
    kr_new = kc[:, MLA_KV_LORA:MLA_KV_LORA + MLA_ROPE].astype(BF16)
    s_new = _dot_nt(qlat, ckv_new) + _dot_nt(qrope, kr_new)
    mla_update(jnp.where(t_k <= t_q, s_new, -jnp.inf), ckv_new)
    z_new = _dot_nt(qsb, pad(sk_ref[...]).astype(BF16))
    w_new, carry0 = _sb_weights(z_new, t_k < t_q, jnp.zeros((rows, 1), F32), su)
    sbacc_sc[...] = _dot(w_new, pad(sv_ref[...]).astype(BF16))
    carry_sc[...] = jnp.broadcast_to(carry0, carry_sc.shape)

    def compute(slot):
        lanes = lambda buf: jnp.concatenate([buf[slot, j] for j in range(ppc)], axis=-1)
        ckv = ckv_buf[slot].astype(BF16)
        s = _dot_nt(qlat, ckv) + _dot(qrope, lanes(kr_buf).astype(BF16))
        mla_update(s, ckv)

        z = _dot(qsb, lanes(sbk_buf).astype(BF16))
        lk = _log_sigmoid_neg(z)
        lk_rows = jnp.concatenate([lk[:, j * page:(j + 1) * page] for j in range(ppc)], axis=0)
        ain = _dot3(lk_rows, su)
        tot = ain[:, 0:1] + lk_rows[:, 0:1]
        carry = carry_sc[...]
        ws = [None] * ppc
        for j in reversed(range(ppc)):
            r = slice(j * rows, (j + 1) * rows)
            ws[j] = jnp.exp(z[:, j * page:(j + 1) * page] + lk_rows[r] + ain[r] + carry).astype(BF16)
            carry = carry + tot[r]
        w = jnp.concatenate(ws, axis=-1)
        sbacc_sc[...] = sbacc_sc[...] + _dot_nt(w, lanes(sbv_buf).astype(BF16))
        carry_sc[...] = carry

    def pair(i, carry):
        c0 = 2 * i
        start_chunk(b, c0 + 1, 1)
        wait_chunk(0)
        compute(0)
        wrap = c0 + 2 == nchunk
        b_next = jnp.where(wrap, jnp.where(b + 1 == nb, 0, b + 1), b)
        c_next = jnp.where(wrap, 0, c0 + 2)
        start_chunk(b_next, c_next, 0)
        wait_chunk(1)
        compute(1)
        return carry

    lax.fori_loop(0, nchunk // 2, pair, 0)

    o = (acc_sc[...] / l_sc[...]).astype(BF16)
    ocat = jnp.concatenate([o[h * ts:(h + 1) * ts] for h in range(MLA_HEADS)], axis=-1)
    ymla_ref[...] = _dot(ocat, wuv_ref[...])
    ysb_ref[...] = _head_unstack(sbacc_sc[...], ts)

    @pl.when(b == nb - 1)
    def _():
        wait_chunk(0)


def _decode(layer, page_table, qcat, kcat, sq, sk, sv, wuv, caches, ppc):
    cache_ckv, cache_kr, cache_sbk, cache_sbv = caches
    nbatch, n_pages = page_table.shape
    page = cache_ckv.shape[2]
    assert cache_kr.shape[2:] == (MLA_ROPE, page) and cache_sbk.shape[2:] == (GROUP_WIDTH, page)
    assert (n_pages // ppc) % 2 == 0
    ts = qcat.shape[1] // nbatch
    rows = MLA_HEADS * ts
    seq = lambda b, pt: (b, 0)
    in_specs = [
        pl.BlockSpec((MLA_HEADS, ts, 256), lambda b, pt: (0, b, 0)),
        pl.BlockSpec((ts, 256), seq), pl.BlockSpec((ts, 256), seq),
        pl.BlockSpec((ts, 256), seq), pl.BlockSpec((ts, 256), seq),
        pl.BlockSpec(wuv.shape, lambda b, pt: (0, 0)),
    ] + [pl.BlockSpec(memory_space=pl.ANY)] * 4
    grid_spec = pltpu.PrefetchScalarGridSpec(
        num_scalar_prefetch=1,
        grid=(nbatch,),
        in_specs=in_specs,
        out_specs=[pl.BlockSpec((ts, 256), seq)] * 2,
        scratch_shapes=[
            pltpu.VMEM((2, ppc * page, MLA_KV_LORA), F32),
            pltpu.VMEM((2, ppc, MLA_ROPE, page), F32),
            pltpu.VMEM((2, ppc, GROUP_WIDTH, page), F32),
            pltpu.VMEM((2, ppc, GROUP_WIDTH, page), F32),
            pltpu.SemaphoreType.DMA((4, 2)),
            pltpu.VMEM((rows, LANES), F32), pltpu.VMEM((rows, LANES), F32),
            pltpu.VMEM((rows, MLA_KV_LORA), F32), pltpu.VMEM((rows, 256), F32),
            pltpu.VMEM((rows, LANES), F32),
        ],
    )
    return pl.pallas_call(
        functools.partial(_decode_kernel, layer=layer, ppc=ppc, ts=ts, page=page, n_pages=n_pages),
        grid_spec=grid_spec,
        out_shape=[jax.ShapeDtypeStruct((nbatch * ts, 256), F32)] * 2,
        compiler_params=_params(("arbitrary",)),
        name="decode",
    )(page_table.reshape(-1), qcat, kcat, sq, sk, sv, wuv, *caches)


def _pad_lanes(w, width=LANES):
    return jnp.pad(w, ((0, 0), (0, width - w.shape[1])))


def _rot_cols(w):
    half = w.shape[-1] // 2
    return jnp.concatenate([-w[..., half:], w[..., :half]], axis=-1)


def _prep_layer(i, p):
    gw = GROUP_WIDTH
    conv_dim = gw + 2 * SSD_GROUPS * SSD_STATE
    sizes = (gw, conv_dim, SSD_HEADS, 256, MLA_KV_LORA, MLA_ROPE, gw, gw, gw, gw)
    offs = [0]
    for s in sizes:
        offs.append(offs[-1] + s)
    w_in = p["w_in"][i]
    z, xbc, dt, cq, ckv, kr, u, sq, sk, sv = [w_in[:, offs[j]:offs[j + 1]] for j in range(10)]
    w_in_ext = jnp.concatenate(
        [z, xbc, cq, ckv, u, sq, sk, sv, _pad_lanes(kr), _pad_lanes(_rot_cols(kr)), _pad_lanes(dt)],
        axis=1).astype(BF16)

    w_uq = p["w_uq"][i].reshape(256, MLA_HEADS, MLA_NOPE + MLA_ROPE)
    nope = w_uq[:, :, :MLA_NOPE].reshape(256, MLA_HEADS * MLA_NOPE)
    rope = w_uq[:, :, MLA_NOPE:]
    padh = lambda w: jnp.pad(w, ((0, 0), (0, 0), (0, LANES - MLA_ROPE))).reshape(256, MLA_HEADS * LANES)
    w_uq_ext = jnp.concatenate([nope, padh(rope), padh(_rot_cols(rope))], axis=1).astype(BF16)
    eye_h = jnp.eye(MLA_HEADS, dtype=F32)
    w_uk_bd = jnp.einsum("chd,hg->hdgc", p["w_uk"][i], eye_h).reshape(
        MLA_HEADS * MLA_NOPE, MLA_HEADS * MLA_KV_LORA).astype(BF16)
    w_uv_bd = jnp.einsum("chv,hg->hcgv", p["w_uv"][i], eye_h).reshape(
        MLA_HEADS * MLA_KV_LORA, GROUP_WIDTH).astype(BF16)

    lr, li = p["lam_re"][i], p["lam_im"][i]
    step = jnp.exp(p["log_step"][i])[:, None]
    mag = jnp.exp(lr * step)
    br, bi = mag * jnp.cos(li * step), mag * jnp.sin(li * step)
    den = lr * lr + li * li
    cr = ((br - 1.0) * lr + bi * li) / den
    ci = (bi * lr - (br - 1.0) * li) / den
    b_re, b_im = p["b_re"][i], p["b_im"][i]
    bbar_re = cr[..., None] * b_re - ci[..., None] * b_im
    bbar_im = cr[..., None] * b_im + ci[..., None] * b_re
    eye_g = jnp.eye(S5_GROUPS, dtype=F32)
    ns = S5_GROUPS * S5_STATE
    to_b = lambda b: jnp.einsum("gpc,gh->gchp", b, eye_g).reshape(S5_GROUPS * S5_GROUP, ns)
    s5_bmat = jnp.concatenate([to_b(bbar_re), to_b(bbar_im)], axis=1).astype(BF16)
    to_c = lambda c: jnp.einsum("gcp,gh->hpgc", c, eye_g).reshape(ns, S5_GROUPS * S5_GROUP)
    s5_cmat = jnp.concatenate([to_c(p["c_re"][i]), -to_c(p["c_im"][i])], axis=0).astype(BF16)
    s5_lam = jnp.stack([br.reshape(ns), bi.reshape(ns)])

    row = lambda v: v.reshape(1, -1)
    return {
        "g_mix": row(p["g_mix"][i]), "w_in_ext": w_in_ext, "g_q": row(p["g_q"][i]),
        "w_uq_ext": w_uq_ext, "w_uk_bd": w_uk_bd, "w_uv_bd": w_uv_bd, "g_kv": row(p["g_kv"][i]),
        "conv_w": p["conv_w"][i], "conv_b": row(p["conv_b"][i]),
        "dt_bias": _pad_lanes(row(p["dt_bias"][i])), "a_log": _pad_lanes(row(p["a_log"][i])),
        "d_exp": row(jnp.repeat(p["d_ssd"][i], SSD_HEAD_DIM)), "g_ssd": row(p["g_ssd"][i]),
        "s5_lam": s5_lam, "s5_bmat": s5_bmat, "s5_cmat": s5_cmat, "d_s5": row(p["d_s5"][i]),
        "w_glu_a": p["w_glu_a"][i].astype(BF16), "w_glu_b": p["w_glu_b"][i].astype(BF16),
        "w_out": p["w_out"][i].astype(BF16), "g_ffn": row(p["g_ffn"][i]),
        "w_gate": p["w_gate"][i].astype(BF16), "w_up": p["w_up"][i].astype(BF16),
        "w_down": p["w_down"][i].astype(BF16), "g_ple": row(p["g_ple"][i]),
        "w_pg": p["w_pg"][i].astype(BF16), "w_ple": p["w_ple"][i].astype(BF16),
    }


def _rope_table(pos):
    inv = ROPE_THETA ** (-jnp.arange(0, MLA_ROPE, 2, dtype=F32) / MLA_ROPE)
    ang = pos.astype(F32)[:, None] * inv[None, :]
    cos, sin = jnp.cos(ang), jnp.sin(ang)
    tile = lambda a: jnp.tile(jnp.concatenate([a, a], axis=-1), (1, LANES // MLA_ROPE))
    return jnp.concatenate([tile(cos), tile(sin)], axis=-1)


def _s5_state_in(h):
    b = h.shape[0]
    return jnp.moveaxis(h.reshape(b, S5_GROUPS * S5_STATE, 2), 2, 1).reshape(b, -1)


def _s5_state_out(h):
    b = h.shape[0]
    return jnp.moveaxis(h.reshape(b, 2, S5_GROUPS * S5_STATE), 1, 2).reshape(
        b, S5_GROUPS, S5_STATE, 2)


def _group_step(x, pemb, tab, lw, g_final, *, nseq, seq_len, conv_prev, ssd_h0, s5_h0,
                attend, tm, ssd_nb, s5_tc, time_major, final):
    tml = seq_len if time_major else None
    (z, xbc, dt, ckv, kr, kcat, qcat, u, sqb, skb, svb, sk, sv) = _inproj(x, tab, lw, tm, tml)
    hp = SSD_HEADS * SSD_HEAD_DIM
    prev8 = jnp.pad(conv_prev, ((0, 0), (SUBLANES - (SSD_CONV - 1), 0), (0, 0)))
    y_ssd, tail, ssd_h = _ssd(z, xbc, dt, prev8, ssd_h0.reshape(nseq, hp, SSD_STATE), lw,
                              nseq, seq_len, ssd_nb)
    if time_major:
        u_tm = u.reshape(seq_len, nseq, 256)
    else:
        u_tm = jnp.swapaxes(u.reshape(nseq, seq_len, 256), 0, 1)
    y_s5_tm, s5_h = _s5(u_tm, _s5_state_in(s5_h0), lw, s5_tc)
    if time_major:
        y_s5 = y_s5_tm.reshape(seq_len, nseq * 256)
    else:
        y_s5 = jnp.swapaxes(y_s5_tm, 0, 1).reshape(nseq * seq_len, 256)
    y_mla, y_sb = attend(qcat, kcat, sqb, skb, svb, sk, sv)
    x_new = _post(x, (y_ssd, y_mla, y_s5, y_sb), pemb, lw, g_final, tm, final, tml)
    states = (
        ckv.reshape(nseq, seq_len, MLA_KV_LORA), kr.reshape(nseq, seq_len, MLA_ROPE),
        sk.reshape(nseq, seq_len, SB_HEADS, SB_HEAD_DIM),
        sv.reshape(nseq, seq_len, SB_HEADS, SB_HEAD_DIM),
        ssd_h.reshape(nseq, SSD_HEADS, SSD_HEAD_DIM, SSD_STATE),
        tail[:, SUBLANES - (SSD_CONV - 1):, :],
        _s5_state_out(s5_h),
    )
    return x_new, states


def kernel(x_prompt, x_sample, p_prompt, p_sample, cache_mla_ckv, cache_mla_krope, cache_sb_k, cache_sb_v, state_ssd, state_ssd_conv, state_s5, page_table, g_mix, w_in, conv_w, conv_b, dt_bias, a_log, d_ssd, g_ssd, g_q, w_uq, g_kv, w_uk, w_uv, lam_re, lam_im, b_re, b_im, c_re, c_im, d_s5, log_step, w_glu_a, w_glu_b, w_out, g_ffn, w_gate, w_up, w_down, g_ple, w_pg, w_ple, g_final):
    params = dict(g_mix=g_mix, w_in=w_in, conv_w=conv_w, conv_b=conv_b, dt_bias=dt_bias,
                  a_log=a_log, d_ssd=d_ssd, g_ssd=g_ssd, g_q=g_q, w_uq=w_uq, g_kv=g_kv,
                  w_uk=w_uk, w_uv=w_uv, lam_re=lam_re, lam_im=lam_im, b_re=b_re, b_im=b_im,
                  c_re=c_re, c_im=c_im, d_s5=d_s5, log_step=log_step, w_glu_a=w_glu_a,
                  w_glu_b=w_glu_b, w_out=w_out, g_ffn=g_ffn, w_gate=w_gate, w_up=w_up,
                  w_down=w_down, g_ple=g_ple, w_pg=w_pg, w_ple=w_ple)
    bp, tp, d = x_prompt.shape
    bs, ts, _ = x_sample.shape
    depth = w_in.shape[0]
    n_pages = page_table.shape[1]
    page = cache_mla_ckv.shape[2]
    past_len = n_pages * page
    n_p, n_s = bp * tp, bs * ts
    assert ts == SUBLANES and tp % SSD_CHUNK == 0 and bp % SUBLANES == 0 and bs % SUBLANES == 0

    tm_p = _pick(tp, (512, 256, 128))
    tm_s = _pick(n_s, (512, 256, 128, 64))
    tq = _pick(tp, (256, 128))
    ppc = _pick(n_pages, (64, 32, 16, 8, 4, 2)) // 2
    ssd_nb_s = _pick(bs, (16, 8))
    s5_tc_p = _pick(tp, (64, 32))

    tab_p = _rope_table(jnp.arange(tp, dtype=jnp.int32))
    tab_s = jnp.tile(_rope_table(past_len + jnp.arange(ts, dtype=jnp.int32)), (tm_s // ts, 1))
    g_fin = g_final.reshape(1, d)
    kv_t = lambda c: jnp.transpose(c, (0, 1, 3, 4, 2)).reshape(c.shape[:2] + (GROUP_WIDTH, page))
    caches = (cache_mla_ckv, jnp.transpose(cache_mla_krope, (0, 1, 3, 2)),
              kv_t(cache_sb_k), kv_t(cache_sb_v))

    xp = x_prompt.reshape(n_p, d)
    xs = x_sample.reshape(n_s, d)
    new_p, new_s = [], []
    for i in range(depth):
        lw = _prep_layer(i, params)

        def attend_prompt(qcat, kcat, sqb, skb, svb, sk, sv, lw=lw):
            return (_mla_prompt(qcat, kcat, lw["w_uv_bd"], bp, tp, tq),
                    _sb_prompt(sqb, skb, svb, bp, tp, tq))

        def attend_sample(qcat, kcat, sqb, skb, svb, sk, sv, lw=lw, i=i):
            return _decode(i, page_table, qcat.astype(F32), kcat.astype(F32), sqb.astype(F32),
                           sk, sv, lw["w_uv_bd"], caches, ppc)

        xp, st_p = _group_step(
            xp, p_prompt[i].reshape(n_p, -1), tab_p, lw, g_fin, nseq=bp, seq_len=tp,
            conv_prev=jnp.zeros((bp, SSD_CONV - 1, 768), F32),
            ssd_h0=jnp.zeros((bp, SSD_HEADS, SSD_HEAD_DIM, SSD_STATE), F32),
            s5_h0=jnp.zeros((bp, S5_GROUPS, S5_STATE, 2), F32),
            attend=attend_prompt, tm=tm_p, ssd_nb=1, s5_tc=s5_tc_p, time_major=True,
            final=i == depth - 1)
        xs, st_s = _group_step(
            xs, p_sample[i].reshape(n_s, -1), tab_s, lw, g_fin, nseq=bs, seq_len=ts,
            conv_prev=state_ssd_conv[i], ssd_h0=state_ssd[i], s5_h0=state_s5[i],
            attend=attend_sample, tm=tm_s, ssd_nb=ssd_nb_s, s5_tc=ts, time_major=False,
            final=i == depth - 1)
        new_p.append(st_p)
        new_s.append(st_s)

    stack = lambda lst: tuple(jnp.stack([st[j] for st in lst]) for j in range(7))
    return ((xp.reshape(bp, tp, d), xs.reshape(bs, ts, d)) + stack(new_p) + stack(new_s))
```
